```python
import jax, jax.numpy as jnp
from jax import lax
import numpy as np

D_MODEL = 1024
BATCH = 8
SEQ = 2048
DEPTH = 4

GRID_W = 64
CTX_LEN = 256
N_MIXERS = 2
N_CONV_LAYERS = (DEPTH + 1) // 2
N_REC_LAYERS = DEPTH // 2
CONV_WIDTH = 31
REC_HEADS = 8
REC_HEAD_DIM = D_MODEL // REC_HEADS
REC_CHUNK = 16
D_FF = ((8 * D_MODEL // 3 + 127) // 128) * 128
N_EXPERTS = 8
TOP_K = 2
EPS = 1e-6

kernel_name = "hybrid_conv_hgrn2_moe_dit_prefix"


def rms_norm(x, g):
    xf = x.astype(jnp.float32)
    y = xf * lax.rsqrt(jnp.mean(xf * xf, axis=-1, keepdims=True) + EPS)
    return (y * g.astype(jnp.float32)).astype(x.dtype)


def layer_norm(x, g, b):
    xf = x.astype(jnp.float32)
    mu = jnp.mean(xf, axis=-1, keepdims=True)
    var = jnp.mean(jnp.square(xf - mu), axis=-1, keepdims=True)
    y = (xf - mu) * lax.rsqrt(var + EPS)
    return (y * g.astype(jnp.float32) + b.astype(jnp.float32)).astype(x.dtype)


def modulate(h, shift, scale):
    return h * (1.0 + scale) + shift


def swiglu(h, w1, w3, w2):
    return (jax.nn.silu(h @ w1) * (h @ w3)) @ w2


def moe_swiglu(h, router, w1, w3, w2):
    logits = jnp.einsum('bld,de->ble', h, router).astype(jnp.float32)
    top_v, top_i = lax.top_k(logits, TOP_K)
    top_w = jax.nn.softmax(top_v, axis=-1)
    gates = jnp.sum(jax.nn.one_hot(top_i, N_EXPERTS, dtype=jnp.float32) * top_w[..., None], axis=-2)
    gates = gates.astype(h.dtype)
    out = jnp.zeros_like(h)
    for e in range(N_EXPERTS):
        out = out + gates[..., e:e + 1] * swiglu(h, w1[e], w3[e], w2[e])
    return out


def dwconv1d(u, w, b):
    pad = (w.shape[0] - 1) // 2
    y = lax.conv_general_dilated(u, w[:, None, :].astype(u.dtype), window_strides=(1,),
                                 padding=[(pad, pad)], dimension_numbers=('NWC', 'WIO', 'NWC'),
                                 feature_group_count=u.shape[-1])
    return y + b


def axial_dwconv(u, w, b):
    bsz, s, d = u.shape
    rows = s // GRID_W
    half = d // 2
    g = u.reshape(bsz, rows, GRID_W, d)
    along_w = dwconv1d(g[..., :half].reshape(bsz * rows, GRID_W, half), w[:, :half], b[:half])
    along_w = along_w.reshape(bsz, rows, GRID_W, half)
    cols = jnp.swapaxes(g[..., half:], 1, 2).reshape(bsz * GRID_W, rows, d - half)
    along_h = dwconv1d(cols, w[:, half:], b[half:]).reshape(bsz, GRID_W, rows, d - half)
    along_h = jnp.swapaxes(along_h, 1, 2)
    return jnp.concatenate([along_w, along_h], axis=-1).reshape(bsz, s, d)


def conv_module(h, dwconv, pw1_w, pw1_b, dw_w, dw_b, ln_g, ln_b, pw2_w, pw2_b):
    a = h @ pw1_w + pw1_b
    d = a.shape[-1] // 2
    u = a[..., :d] * jax.nn.sigmoid(a[..., d:])
    u = dwconv(u, dw_w, dw_b)
    u = jax.nn.silu(layer_norm(u, ln_g, ln_b))
    return u @ pw2_w + pw2_b


def _heads(t):
    return t.reshape(t.shape[0], t.shape[1], REC_HEADS, REC_HEAD_DIM).astype(jnp.float32)


def _forget(z, lb):
    f = lb + (1.0 - lb) * jax.nn.sigmoid(_heads(z))
    return 1.0 - f, jnp.log(f)


def gla_chunked(q, k, v, logf, s0):
    bsz, length, nh, dk = q.shape
    dv = v.shape[-1]
    n = length // REC_CHUNK
    c = REC_CHUNK
    rs = lambda t: t.reshape(bsz, n, c, nh, t.shape[-1])
    q, k, v, logf = rs(q), rs(k), rs(v), rs(logf)
    bcum = jnp.cumsum(logf, axis=2)
    b_last = bcum[:, :, -1:]
    q_in = q * jnp.exp(bcum)
    k_in = k * jnp.exp(-bcum)
    k_st = k * jnp.exp(b_last - bcum)
    mask = jnp.tril(jnp.ones((c, c), dtype=bool))
    att = jnp.where(mask, jnp.einsum('bnthd,bnshd->bnhts', q_in, k_in), 0.0)
    o_intra = jnp.einsum('bnhts,bnshe->bnthe', att, v)

    def step(s, inp):
        q_j, k_j, v_j, dec_j = inp
        o_j = jnp.einsum('bthd,bhde->bthe', q_j, s)
        s = dec_j[..., None] * s + jnp.einsum('bshd,bshe->bhde', k_j, v_j)
        return s, o_j

    xs = (jnp.moveaxis(q_in, 1, 0), jnp.moveaxis(k_st, 1, 0), jnp.moveaxis(v, 1, 0),
          jnp.moveaxis(jnp.exp(b_last[:, :, 0]), 1, 0))
    s_final, o_inter = lax.scan(step, s0, xs)
    o = o_intra + jnp.moveaxis(o_inter, 0, 1)
    return o.reshape(bsz, length, nh, dv), s_final


def gla_final_state(k, v, logf):
    bcum = jnp.cumsum(logf, axis=1)
    return jnp.einsum('blhd,blhe->bhde', k * jnp.exp(bcum[:, -1:] - bcum), v)


def hgrn2_mixer(h_lat, h_ctx, w_in, lb_fwd, lb_bwd, onorm_g, w_o, ctx_out):
    d = h_lat.shape[-1]
    lbf = lb_fwd.reshape(REC_HEADS, REC_HEAD_DIM)
    lbb = lb_bwd.reshape(REC_HEADS, REC_HEAD_DIM)
    flip = lambda t: jnp.flip(t, axis=1)

    def readout(o, g):
        o = o * lax.rsqrt(jnp.mean(o * o, axis=-1, keepdims=True) + EPS)
        o = o * onorm_g.reshape(REC_HEADS, REC_HEAD_DIM).astype(jnp.float32)
        o = o.reshape(o.shape[0], o.shape[1], d).astype(g.dtype) * jax.nn.silu(g)
        return o @ w_o

    def project_state(p):
        i_, zf, zb = p[..., :d], p[..., d:2 * d], p[..., 2 * d:3 * d]
        kf, lf = _forget(zf, lbf)
        kb, lb = _forget(zb, lbb)
        return _heads(i_), kf, lf, kb, lb

    if ctx_out:
        p_ctx = h_ctx @ w_in
        vc, kfc, lfc, kbc, lbc = project_state(p_ctx)
        qc = jax.nn.silu(_heads(p_ctx[..., 3 * d:4 * d])) * (REC_HEAD_DIM ** -0.5)
        s_zero = jnp.zeros((h_ctx.shape[0], REC_HEADS, REC_HEAD_DIM, REC_HEAD_DIM), jnp.float32)
        oc_f, sc_f = gla_chunked(qc, kfc, vc, lfc, s_zero)
        oc_b, sc_b = gla_chunked(flip(qc), flip(kbc), flip(vc), flip(lbc), s_zero)
        y_ctx = readout(oc_f + flip(oc_b), p_ctx[..., 4 * d:])
    else:
        vc, kfc, lfc, kbc, lbc = project_state(h_ctx @ w_in[:, :3 * d])
        sc_f = gla_final_state(kfc, vc, lfc)
        sc_b = gla_final_state(flip(kbc), flip(vc), flip(lbc))
        y_ctx = None

    p_lat = h_lat @ w_in
    vl, kfl, lfl, kbl, lbl = project_state(p_lat)
    ql = jax.nn.silu(_heads(p_lat[..., 3 * d:4 * d])) * (REC_HEAD_DIM ** -0.5)
    ol_f, _ = gla_chunked(ql, kfl, vl, lfl, sc_f)
    ol_b, _ = gla_chunked(flip(ql), flip(kbl), flip(vl), flip(lbl), sc_b)
    y_lat = readout(ol_f + flip(ol_b), p_lat[..., 4 * d:])
    return y_lat, y_ctx


def setup_inputs(seed: int = 0) -> dict:
    key = jax.random.key(seed)
    ks = jax.random.split(key, 32)
    d, f = D_MODEL, D_FF
    nrm = lambda k, shape, s: jax.random.normal(k, shape, jnp.float32) * s
    return {
        "x": nrm(ks[0], (BATCH, SEQ, d), 1.0),
        "c": nrm(ks[1], (BATCH, d), 1.0),
        "ctx": nrm(ks[2], (BATCH, CTX_LEN, d), 1.0),
        "c_ctx": nrm(ks[3], (d,), 1.0),
        "ada_w": nrm(ks[4], (DEPTH, d, 6 * d), 0.5 * d ** -0.5),
        "ada_b": nrm(ks[5], (DEPTH, 6 * d), 0.02),
        "norm_mix_g": 1.0 + nrm(ks[6], (DEPTH, d), 0.02),
        "norm_ffn_g": 1.0 + nrm(ks[7], (DEPTH, d), 0.02),
        "final_g": 1.0 + nrm(ks[8], (d,), 0.02),
        "conv_pw1_w": nrm(ks[9], (N_CONV_LAYERS, d, 2 * d), d ** -0.5),
        "conv_pw1_b": nrm(ks[10], (N_CONV_LAYERS, 2 * d), 0.02),
        "conv_dw_w": nrm(ks[11], (N_CONV_LAYERS, CONV_WIDTH, d), CONV_WIDTH ** -0.5),
        "conv_dw_b": nrm(ks[12], (N_CONV_LAYERS, d), 0.02),
        "conv_ln_g": 1.0 + nrm(ks[13], (N_CONV_LAYERS, d), 0.02),
        "conv_ln_b": nrm(ks[14], (N_CONV_LAYERS, d), 0.02),
        "conv_pw2_w": nrm(ks[15], (N_CONV_LAYERS, d, d), d ** -0.5),
        "conv_pw2_b": nrm(ks[16], (N_CONV_LAYERS, d), 0.02),
        "rec_w_in": nrm(ks[17], (N_REC_LAYERS, d, 5 * d), d ** -0.5),
        "rec_lb_logits": nrm(ks[18], (2, DEPTH, d), 0.1),
        "rec_onorm_g": 1.0 + nrm(ks[19], (N_REC_LAYERS, d), 0.02),
        "rec_w_o": nrm(ks[20], (N_REC_LAYERS, d, d), d ** -0.5),
        "ffn_w1": nrm(ks[21], (N_CONV_LAYERS, d, f), d ** -0.5),
        "ffn_w3": nrm(ks[22], (N_CONV_LAYERS, d, f), d ** -0.5),
        "ffn_w2": nrm(ks[23], (N_CONV_LAYERS, f, d), f ** -0.5),
        "moe_router": nrm(ks[24], (N_REC_LAYERS, d, N_EXPERTS), d ** -0.5),
        "moe_w1": nrm(ks[25], (N_REC_LAYERS, N_EXPERTS, d, f), d ** -0.5),
        "moe_w3": nrm(ks[26], (N_REC_LAYERS, N_EXPERTS, d, f), d ** -0.5),
        "moe_w2": nrm(ks[27], (N_REC_LAYERS, N_EXPERTS, f, d), f ** -0.5),
    }


def reference(x, c, ctx, c_ctx, ada_w, ada_b, norm_mix_g, norm_ffn_g, final_g,
              conv_pw1_w, conv_pw1_b, conv_dw_w, conv_dw_b, conv_ln_g, conv_ln_b, conv_pw2_w, conv_pw2_b,
              rec_w_in, rec_lb_logits, rec_onorm_g, rec_w_o,
              ffn_w1, ffn_w3, ffn_w2,
              moe_router, moe_w1, moe_w3, moe_w2):
    mod_lat_all = jnp.einsum('bd,lde->lbe', jax.nn.silu(c), ada_w) + ada_b[:, None, :]
    mod_ctx_all = jnp.einsum('d,lde->le', jax.nn.silu(c_ctx), ada_w) + ada_b
    sm = jax.nn.softmax(rec_lb_logits.astype(jnp.float32), axis=1)
    lower = jnp.cumsum(sm, axis=1) - sm[:, :1]

    h_lat, h_ctx = x, ctx
    for i in range(DEPTH):
        last = i == DEPTH - 1
        j = i // N_MIXERS
        ml = jnp.split(mod_lat_all[i][:, None, :], 6, axis=-1)
        mc = jnp.split(mod_ctx_all[i], 6, axis=-1)

        a_lat = modulate(rms_norm(h_lat, norm_mix_g[i]), ml[0], ml[1])
        a_ctx = modulate(rms_norm(h_ctx, norm_mix_g[i]), mc[0], mc[1])
        if i % N_MIXERS == 0:
            cp = (conv_pw1_w[j], conv_pw1_b[j], conv_dw_w[j], conv_dw_b[j],
                  conv_ln_g[j], conv_ln_b[j], conv_pw2_w[j], conv_pw2_b[j])
            y_lat = conv_module(a_lat, axial_dwconv, *cp)
            y_ctx = None if last else conv_module(a_ctx, dwconv1d, *cp)
        else:
            y_lat, y_ctx = hgrn2_mixer(a_lat, a_ctx, rec_w_in[j], lower[0, i], lower[1, i],
                                       rec_onorm_g[j], rec_w_o[j], not last)
        h_lat = h_lat + ml[2] * y_lat
        if not last:
            h_ctx = h_ctx + mc[2] * y_ctx

        f_lat = modulate(rms_norm(h_lat, norm_ffn_g[i]), ml[3], ml[4])
        if last:
            tokens = f_lat
        else:
            f_ctx = modulate(rms_norm(h_ctx, norm_ffn_g[i]), mc[3], mc[4])
            tokens = jnp.concatenate([f_lat, f_ctx], axis=1)
        if i % 2 == 0:
            y = swiglu(tokens, ffn_w1[j], ffn_w3[j], ffn_w2[j])
        else:
            y = moe_swiglu(tokens, moe_router[j], moe_w1[j], moe_w3[j], moe_w2[j])
        s_lat = h_lat.shape[1]
        h_lat = h_lat + ml[5] * y[:, :s_lat]
        if not last:
            h_ctx = h_ctx + mc[5] * y[:, s_lat:]

    return rms_norm(h_lat, final_g)
```

```python
import functools

import numpy as np
import jax
import jax.numpy as jnp
from jax import lax
from jax.experimental import pallas as pl
from jax.experimental.pallas import tpu as pltpu

EPS = 1e-6
GRID_W = 64
CONV_PAD = 16
HEAD_DIM = 128
BLK = 128
CHUNK = 16
TOP_K = 2
TM = 256
LANES = 128
VMEM_LIMIT = 56 * 1024 * 1024

F32 = jnp.float32
BF16 = jnp.bfloat16


def _cp(*sem):
    return pltpu.CompilerParams(dimension_semantics=sem, vmem_limit_bytes=VMEM_LIMIT)


def _dot(a, b):
    return jnp.dot(a, b, preferred_element_type=F32)


def _dot_nt(a, b):
    return lax.dot_general(a, b, (((1,), (1,)), ((), ())), preferred_element_type=F32)


def _dot_tn(a, b):
    return lax.dot_general(a, b, (((0,), (0,)), ((), ())), preferred_element_type=F32)


def _sigmoid(x):
    return 1.0 / (1.0 + jnp.exp(-x))


def _silu(x):
    return x * _sigmoid(x)


def _rms_mod(x, g, shift, scale):
    ms = jnp.mean(x * x, axis=-1, keepdims=True)
    return (x * lax.rsqrt(ms + EPS) * g) * (1.0 + scale) + shift


def _mod_kernel(cond_ref, w_ref, b_ref, o_ref):
    s = _silu(cond_ref[...])
    o_ref[0] = jnp.dot(s, w_ref[0], preferred_element_type=F32,
                       precision=lax.Precision.HIGHEST) + b_ref[0]


def _modulation(cond, ada_w, ada_b):
    depth, d, n = ada_w.shape
    tn = n // 4
    out = pl.pallas_call(
        _mod_kernel,
        out_shape=jax.ShapeDtypeStruct((depth, cond.shape[0], n), F32),
        grid=(depth, n // tn),
        in_specs=[
            pl.BlockSpec(cond.shape, lambda l, j: (0, 0)),
            pl.BlockSpec((1, d, tn), lambda l, j: (l, 0, j)),
            pl.BlockSpec((1, 1, tn), lambda l, j: (l, 0, j)),
        ],
        out_specs=pl.BlockSpec((1, cond.shape[0], tn), lambda l, j: (l, 0, j)),
        compiler_params=_cp("arbitrary", "arbitrary"),
        name="modulation",
    )(cond, ada_w, ada_b.reshape(depth, 1, n))
    return out.reshape(depth, cond.shape[0], 6, d)


def _row_specs(nb, d, t0, ctx_tiles):
    tok = lambda width: pl.BlockSpec((1, TM, width), lambda b, t: (b, t + t0, 0))
    mod = pl.BlockSpec((1, 6, d), lambda b, t: (jnp.where(t + t0 < ctx_tiles, nb, b), 0, 0))
    return tok, mod


def _full(shape):
    return pl.BlockSpec(shape, lambda b, t: (0,) * len(shape))


def _pw1_kernel(x_ref, mod_ref, g_ref, w_ref, b_ref, u_ref):
    m = mod_ref[0]
    a = _rms_mod(x_ref[0], g_ref[...], m[0:1], m[1:2]).astype(BF16)
    d = u_ref.shape[-1]
    cw = 512
    for c in range(0, d, cw):
        a1 = _dot(a, w_ref[:, c:c + cw]) + b_ref[:, c:c + cw]
        a2 = _dot(a, w_ref[:, d + c:d + c + cw]) + b_ref[:, d + c:d + c + cw]
        u_ref[0, :, c:c + cw] = a1 * _sigmoid(a2)


def _pw1_glu(h, mods, g, w, b, ctx_tiles):
    nb, l, d = h.shape
    tok, mod = _row_specs(nb, d, 0, ctx_tiles)
    return pl.pallas_call(
        _pw1_kernel,
        out_shape=jax.ShapeDtypeStruct((nb, l, d), F32),
        grid=(nb, l // TM),
        in_specs=[tok(d), mod, _full((1, d)), _full((d, 2 * d)), _full((1, 2 * d))],
        out_specs=tok(d),
        compiler_params=_cp("parallel", "arbitrary"),
        name="conv_pw1_glu",
    )(h, mods, g.reshape(1, d), w.astype(BF16), b.reshape(1, 2 * d))


def _conv_kernel(u_ref, w_ref, b_ref, v_ref, pad_ref, *, ctx, rows, width_tiles):
    kw = w_ref.shape[0]
    half = (kw - 1) // 2
    cw = u_ref.shape[-1]
    j = pl.program_id(1)
    bias = jnp.broadcast_to(b_ref[...], (GRID_W, cw))
    zeros = jnp.zeros((CONV_PAD, cw), F32)

    def line_taps(win):
        acc = bias
        for k in range(kw):
            s = CONV_PAD + k - half
            acc = acc + w_ref[k:k + 1, :] * win[s:s + GRID_W]
        return acc

    pad_ref[0:CONV_PAD] = zeros
    pad_ref[CONV_PAD:CONV_PAD + ctx] = u_ref[0, 0:ctx]
    pad_ref[CONV_PAD + ctx:2 * CONV_PAD + ctx] = zeros
    for c0 in range(0, ctx, GRID_W):
        v_ref[0, c0:c0 + GRID_W] = line_taps(pad_ref[c0:c0 + GRID_W + 2 * CONV_PAD])

    @pl.when(j < width_tiles)
    def _along_width():
        stride = GRID_W + CONV_PAD
        pad_ref[0:CONV_PAD] = zeros
        for r in range(rows):
            base = CONV_PAD + r * stride
            pad_ref[base:base + GRID_W] = u_ref[0, ctx + r * GRID_W:ctx + (r + 1) * GRID_W]
            pad_ref[base + GRID_W:base + stride] = zeros

        def body(r, carry):
            base = pl.multiple_of(r * stride, 8)
            win = pad_ref[pl.ds(base, GRID_W + 2 * CONV_PAD)]
            out0 = pl.multiple_of(ctx + r * GRID_W, 8)
            v_ref[0, pl.ds(out0, GRID_W)] = line_taps(win)
            return carry

        lax.fori_loop(0, rows, body, 0)

    @pl.when(j >= width_tiles)
    def _along_height():
        npad = half * GRID_W
        zrow = jnp.zeros((GRID_W, cw), F32)
        for r in range(half):
            pad_ref[r * GRID_W:(r + 1) * GRID_W] = zrow
            lo = npad + (rows + r) * GRID_W
            pad_ref[lo:lo + GRID_W] = zrow
        pad_ref[npad:npad + rows * GRID_W] = u_ref[0, ctx:ctx + rows * GRID_W]

        def body(r, carry):
            acc = bias
            for k in range(kw):
                src = pl.multiple_of((r + k) * GRID_W, 8)
                acc = acc + w_ref[k:k + 1, :] * pad_ref[pl.ds(src, GRID_W)]
            out0 = pl.multiple_of(ctx + r * GRID_W, 8)
            v_ref[0, pl.ds(out0, GRID_W)] = acc
            return carry

        lax.fori_loop(0, rows, body, 0)


def _dwconv(u, w, b, ctx):
    nb, l, d = u.shape
    kw = w.shape[0]
    cw = 256
    rows = (l - ctx) // GRID_W
    half = (kw - 1) // 2
    pad_rows = max(2 * CONV_PAD + ctx, CONV_PAD + rows * (GRID_W + CONV_PAD),
                   (rows + 2 * half) * GRID_W)
    kern = functools.partial(_conv_kernel, ctx=ctx, rows=rows, width_tiles=(d // 2) // cw)
    return pl.pallas_call(
        kern,
        out_shape=jax.ShapeDtypeStruct((nb, l, d), F32),
        grid=(nb, d // cw),
        in_specs=[
            pl.BlockSpec((1, l, cw), lambda b_, j: (b_, 0, j)),
            pl.BlockSpec((kw, cw), lambda b_, j: (0, j)),
            pl.BlockSpec((1, cw), lambda b_, j: (0, j)),
        ],
        out_specs=pl.BlockSpec((1, l, cw), lambda b_, j: (b_, 0, j)),
        scratch_shapes=[pltpu.VMEM((pad_rows, cw), F32)],
        compiler_params=_cp("parallel", "arbitrary"),
        name="axial_dwconv",
    )(u, w, b.reshape(1, d))


def _pw2_kernel(v_ref, h_ref, mod_ref, lng_ref, lnb_ref, w_ref, b_ref, o_ref):
    v = v_ref[0]
    mu = jnp.mean(v, axis=-1, keepdims=True)
    vc = v - mu
    var = jnp.mean(vc * vc, axis=-1, keepdims=True)
    y = _silu(vc * lax.rsqrt(var + EPS) * lng_ref[...] + lnb_ref[...]).astype(BF16)
    out = _dot(y, w_ref[...]) + b_ref[...]
    o_ref[0] = h_ref[0] + mod_ref[0][2:3] * out


def _pw2_residual(v, h, mods, ln_g, ln_b, w, b, ctx_tiles, t0):
    nb, l, d = h.shape
    tok, mod = _row_specs(nb, d, t0, ctx_tiles)
    return pl.pallas_call(
        _pw2_kernel,
        out_shape=jax.ShapeDtypeStruct((nb, l, d), F32),
        grid=(nb, l // TM - t0),
        in_specs=[tok(d), tok(d), mod, _full((1, d)), _full((1, d)), _full((d, d)), _full((1, d))],
        out_specs=tok(d),
        input_output_aliases={1: 0},
        compiler_params=_cp("parallel", "arbitrary"),
        name="conv_pw2_residual",
    )(v, h, mods, ln_g.reshape(1, d), ln_b.reshape(1, d), w.astype(BF16), b.reshape(1, d))


def _swiglu_hidden(a, w1_ref, w3_ref, hm_ref):
    f = hm_ref.shape[-1]
    cw = 256
    for c in range(0, f, cw):
        h1 = _dot(a, w1_ref[:, c:c + cw])
        h3 = _dot(a, w3_ref[:, c:c + cw])
        hm_ref[:, c:c + cw] = (_silu(h1) * h3).astype(BF16)


def _ffn_kernel(h_ref, mod_ref, g_ref, w1_ref, w3_ref, w2_ref, o_ref, hm_ref):
    m = mod_ref[0]
    h = h_ref[0]
    a = _rms_mod(h, g_ref[...], m[3:4], m[4:5]).astype(BF16)
    _swiglu_hidden(a, w1_ref, w3_ref, hm_ref)
    o_ref[0] = h + m[5:6] * _dot(hm_ref[...], w2_ref[...])


def _ffn_dense(h, mods, g, w1, w3, w2, ctx_tiles):
    nb, l, d = h.shape
    f = w1.shape[-1]
    tok, mod = _row_specs(nb, d, 0, ctx_tiles)
    return pl.pallas_call(
        _ffn_kernel,
        out_shape=jax.ShapeDtypeStruct((nb, l, d), F32),
        grid=(nb, l // TM),
        in_specs=[tok(d), mod, _full((1, d)), _full((d, f)), _full((d, f)), _full((f, d))],
        out_specs=tok(d),
        scratch_shapes=[pltpu.VMEM((TM, f), BF16)],
        input_output_aliases={0: 0},
        compiler_params=_cp("parallel", "arbitrary"),
        name="ffn_swiglu",
    )(h, mods, g.reshape(1, d), w1.astype(BF16), w3.astype(BF16), w2.astype(BF16))


def _win_kernel(x_ref, mod_ref, g_ref, w_ref, lb_ref, p_ref):
    m = mod_ref[0]
    a = _rms_mod(x_ref[0], g_ref[...], m[0:1], m[1:2]).astype(BF16)
    d = x_ref.shape[-1]
    cw = 512
    for c in range(0, 5 * d, cw):
        part, off = divmod(c, d)
        y = _dot(a, w_ref[:, c:c + cw])
        if part in (1, 2):
            lb = lb_ref[part - 1:part, off:off + cw]
            y = jnp.log(lb + (1.0 - lb) * _sigmoid(y))
        elif part == 3:
            y = _silu(y) * (HEAD_DIM ** -0.5)
        elif part == 4:
            y = _silu(y)
        p_ref[0, :, c:c + cw] = y


def _rec_in_proj(h, mods, g, w, lower, ctx_tiles):
    nb, l, d = h.shape
    n = w.shape[-1]
    tok, mod = _row_specs(nb, d, 0, ctx_tiles)
    return pl.pallas_call(
        _win_kernel,
        out_shape=jax.ShapeDtypeStruct((nb, l, n), F32),
        grid=(nb, l // TM),
        in_specs=[tok(d), mod, _full((1, d)), _full((d, n)), _full((2, d))],
        out_specs=tok(n),
        compiler_params=_cp("parallel", "arbitrary"),
        name="rec_in_proj",
    )(h, mods, g.reshape(1, d), w.astype(BF16), lower)


N_CHUNKS = BLK // CHUNK
N_CHUNK_SUMS = 7
SUM_ROWS = BLK + N_CHUNK_SUMS * N_CHUNKS * 8 + 8


def _rec_constants():
    t = np.arange(BLK)
    mats, masks = [], []
    for direction in range(2):
        p = t if direction == 0 else BLK - 1 - t
        pt, ps = p[:, None], p[None, :]
        ct, cs = pt // CHUNK, ps // CHUNK
        same = ct == cs
        mid4 = N_CHUNKS // 2
        mid2 = (ct // 4) * 4 + 2
        chunk_level = [
            same,
            cs < ct,
            cs > ct,
            (ct >= mid4) & (cs >= mid4) & (cs < ct),
            (ct < mid4) & (cs > ct) & (cs < mid4),
            (ct >= mid2) & (cs >= mid2) & (cs < ct),
            (ct < mid2) & (cs > ct) & (cs < mid2),
        ]
        rows = [same & (ps <= pt)]
        for m in chunk_level:
            m = np.broadcast_to(m, (BLK, BLK))
            rows.append(np.repeat(m[::CHUNK], 8, axis=0))
        rows.append(np.ones((8, BLK), bool))
        mats.append(np.concatenate([r.astype(np.float32) for r in rows], axis=0))
        m = [
            same & (ps <= pt),
            (ct % 2 == 1) & (cs == ct - 1),
            (ct // 4 == cs // 4) & (ct % 4 >= 2) & (cs % 4 < 2),
            (ct >= mid4) & (cs < mid4),
        ]
        masks.append(np.stack([np.broadcast_to(x, (BLK, BLK)).astype(np.float32) for x in m], axis=0))
    return np.stack(mats), np.stack(masks)


def _chunk_scale(x, ec, i):
    base = i * N_CHUNKS * 8
    parts = []
    for j in range(N_CHUNKS):
        e = ec[base + 8 * j:base + 8 * j + 8]
        parts.append(x[CHUNK * j:CHUNK * j + 8] * e)
        parts.append(x[CHUNK * j + 8:CHUNK * j + 16] * e)
    return jnp.concatenate(parts, axis=0)


def _gla_kernel(v_ref, lff_ref, lfb_ref, q_ref, a_ref, m_ref, of_ref, ob_ref, *, ctx_blocks, n_blocks):
    last = n_blocks - 1
    n_heads = v_ref.shape[-1] // HEAD_DIM
    chains = [(dr, hd) for dr in range(2) for hd in range(n_heads)]
    lf_refs = (lff_ref, lfb_ref)
    o_refs = (of_ref, ob_ref)

    def step(n, states):
        rev = jnp.where(n < ctx_blocks, ctx_blocks - 1 - n, last + ctx_blocks - n)
        row0 = (pl.multiple_of(n * BLK, BLK), pl.multiple_of(rev * BLK, BLK))

        sums = []
        for dr in range(2):
            logf = lf_refs[dr][0, pl.ds(row0[dr], BLK), :]
            hi = logf.astype(BF16)
            lo = (logf - hi.astype(F32)).astype(BF16)
            sums.append((logf, _dot(a_ref[dr], jnp.concatenate([hi, lo], axis=0))))

        ops = []
        for dr, hd in chains:
            cols = slice(hd * HEAD_DIM, (hd + 1) * HEAD_DIM)
            logf, s = sums[dr]
            s = s[:, cols]
            k = 1.0 - jnp.exp(logf[:, cols])
            qs = q_ref[0, pl.ds(row0[dr], BLK), cols]
            bcum = s[0:BLK]
            ec = jnp.exp(s[BLK:SUM_ROWS])
            q_in = qs * jnp.exp(bcum)
            k_in = k * jnp.exp(-bcum)
            k_st = _chunk_scale(k_in, ec, 0)
            ops.append(dict(
                q_blk=_chunk_scale(q_in, ec, 1).astype(BF16),
                k_end=_chunk_scale(k_st, ec, 2).astype(BF16),
                q4=_chunk_scale(q_in, ec, 3).astype(BF16),
                k4=_chunk_scale(k_st, ec, 4).astype(BF16),
                q2=_chunk_scale(q_in, ec, 5).astype(BF16),
                k2=_chunk_scale(k_st, ec, 6).astype(BF16),
                e_tot=ec[N_CHUNK_SUMS * N_CHUNKS * 8:N_CHUNK_SUMS * N_CHUNKS * 8 + 1],
                q_in=q_in.astype(BF16), k_in=k_in.astype(BF16), k_st=k_st.astype(BF16),
                vb=v_ref[0, pl.ds(row0[dr], BLK), cols].astype(BF16)))

        raw = [(_dot_nt(o["q_in"], o["k_in"]), _dot_nt(o["q_in"], o["k_st"]),
                _dot_nt(o["q2"], o["k2"]), _dot_nt(o["q4"], o["k4"]),
                _dot_nt(o["q_blk"], st.astype(BF16)), _dot_tn(o["vb"], o["k_end"]))
               for o, st in zip(ops, states)]

        atts = []
        for (dr, hd), r in zip(chains, raw):
            att = jnp.where(m_ref[dr, 0] > 0.5, r[0], 0.0)
            att = att + jnp.where(m_ref[dr, 1] > 0.5, r[1], 0.0)
            att = att + jnp.where(m_ref[dr, 2] > 0.5, r[2], 0.0)
            att = att + jnp.where(m_ref[dr, 3] > 0.5, r[3], 0.0)
            atts.append(att.astype(BF16))

        new_states = []
        for (dr, hd), o, r, att, st in zip(chains, ops, raw, atts, states):
            cols = slice(hd * HEAD_DIM, (hd + 1) * HEAD_DIM)
            o_refs[dr][0, pl.ds(row0[dr], BLK), cols] = _dot(att, o["vb"]) + r[4]
            new_states.append(o["e_tot"] * st + r[5])
        return tuple(new_states)

    zero = jnp.zeros((HEAD_DIM, HEAD_DIM), F32)
    lax.fori_loop(0, n_blocks, step, (zero,) * len(chains))


GLA_HEADS_PER_STEP = 2


def _gla(p, ctx):
    nb, l, n5 = p.shape
    d = n5 // 5
    cw = GLA_HEADS_PER_STEP * HEAD_DIM
    groups = d // cw
    amat, masks = _rec_constants()
    amat = np.concatenate([amat, amat], axis=2)
    kern = functools.partial(_gla_kernel, ctx_blocks=ctx // BLK, n_blocks=l // BLK)
    col = lambda base: pl.BlockSpec((1, l, cw), lambda b, h: (b, 0, base + h))
    out = jax.ShapeDtypeStruct((nb, l, d), F32)
    return pl.pallas_call(
        kern,
        out_shape=(out, out),
        grid=(nb, groups),
        in_specs=[col(0), col(groups), col(2 * groups), col(3 * groups),
                  _full(amat.shape), _full(masks.shape)],
        out_specs=(col(0), col(0)),
        compiler_params=_cp("parallel", "arbitrary"),
        name="hgrn2_gla",
    )(p, p, p, p, jnp.asarray(amat, BF16), jnp.asarray(masks, F32))


def _readout_kernel(of_ref, ob_ref, g_ref, h_ref, mod_ref, ng_ref, w_ref, out_ref, y_ref):
    d = of_ref.shape[-1]
    for c in range(0, d, HEAD_DIM):
        o = of_ref[0, :, c:c + HEAD_DIM] + ob_ref[0, :, c:c + HEAD_DIM]
        o = o * lax.rsqrt(jnp.mean(o * o, axis=-1, keepdims=True) + EPS) * ng_ref[:, c:c + HEAD_DIM]
        y_ref[:, c:c + HEAD_DIM] = (o * g_ref[0, :, c:c + HEAD_DIM]).astype(BF16)
    out_ref[0] = h_ref[0] + mod_ref[0][2:3] * _dot(y_ref[...], w_ref[...])


def _rec_readout(o_f, o_b, p, h, mods, onorm_g, w_o, ctx_tiles, t0):
    nb, l, d = h.shape
    tok, mod = _row_specs(nb, d, t0, ctx_tiles)
    gate_cols = pl.BlockSpec((1, TM, d), lambda b, t: (b, t + t0, 4))
    return pl.pallas_call(
        _readout_kernel,
        out_shape=jax.ShapeDtypeStruct((nb, l, d), F32),
        grid=(nb, l // TM - t0),
        in_specs=[tok(d), tok(d), gate_cols, tok(d), mod, _full((1, d)), _full((d, d))],
        out_specs=tok(d),
        scratch_shapes=[pltpu.VMEM((TM, d), BF16)],
        input_output_aliases={3: 0},
        compiler_params=_cp("parallel", "arbitrary"),
        name="rec_readout",
    )(o_f, o_b, p, h, mods, onorm_g.reshape(1, d), w_o.astype(BF16))


def _route_kernel(h_ref, mod_ref, g_ref, r_ref, tri_ref, a_ref, ri_ref, rw_ref, cnt_ref, carry_ref,
                  *, n_experts):
    @pl.when((pl.program_id(0) == 0) & (pl.program_id(1) == 0))
    def _():
        carry_ref[...] = jnp.zeros_like(carry_ref)

    m = mod_ref[0]
    a = _rms_mod(h_ref[0], g_ref[...], m[3:4], m[4:5])
    a_ref[0] = a
    logits = jnp.dot(a, r_ref[...], preferred_element_type=F32, precision=lax.Precision.HIGHEST)
    lane = lax.broadcasted_iota(jnp.int32, logits.shape, 1)
    neg = jnp.float32(-jnp.inf)
    logits = jnp.where(lane < n_experts, logits, neg)
    v1 = jnp.max(logits, axis=-1, keepdims=True)
    i1 = jnp.min(jnp.where(logits == v1, lane, LANES), axis=-1, keepdims=True)
    rest = jnp.where(lane == i1, neg, logits)
    v2 = jnp.max(rest, axis=-1, keepdims=True)
    i2 = jnp.min(jnp.where(rest == v2, lane, LANES), axis=-1, keepdims=True)
    e2 = jnp.exp(v2 - v1)
    w1 = 1.0 / (1.0 + e2)
    w2 = e2 / (1.0 + e2)
    oh1 = jnp.where(lane == i1, 1.0, 0.0)
    oh2 = jnp.where(lane == i2, 1.0, 0.0)
    both = oh1 + oh2
    earlier = _dot(tri_ref[...], both.astype(BF16)) + carry_ref[...]
    r1 = jnp.sum(oh1 * earlier, axis=-1, keepdims=True).astype(jnp.int32)
    r2 = jnp.sum(oh2 * earlier, axis=-1, keepdims=True).astype(jnp.int32)
    carry = carry_ref[...] + jnp.sum(both, axis=0, keepdims=True)
    carry_ref[...] = carry
    cnt_ref[...] = jnp.broadcast_to(carry, cnt_ref.shape).astype(jnp.int32)
    ri_ref[0] = jnp.where(lane == 0, i1, jnp.where(lane == 1, i2, jnp.where(lane == 2, r1,
                          jnp.where(lane == 3, r2, 0))))
    rw_ref[0] = jnp.where(lane == 0, w1, jnp.where(lane == 1, w2, 0.0))


def _moe_route(h, mods, g, router, ctx_tiles, t0):
    nb, l, d = h.shape
    n_experts = router.shape[-1]
    nt = l // TM - t0
    tok, mod = _row_specs(nb, d, t0, ctx_tiles)
    out_tok = lambda width: pl.BlockSpec((1, TM, width), lambda b, t: (b, t, 0))
    rpad = jnp.pad(router, ((0, 0), (0, LANES - n_experts)))
    tri = jnp.asarray(np.tril(np.ones((TM, TM), np.float32), -1), BF16)
    return pl.pallas_call(
        functools.partial(_route_kernel, n_experts=n_experts),
        out_shape=(jax.ShapeDtypeStruct((nb, nt * TM, d), F32),
                   jax.ShapeDtypeStruct((nb, nt * TM, LANES), jnp.int32),
                   jax.ShapeDtypeStruct((nb, nt * TM, LANES), F32),
                   jax.ShapeDtypeStruct((8, LANES), jnp.int32)),
        grid=(nb, nt),
        in_specs=[tok(d), mod, _full((1, d)), _full((d, LANES)), _full((TM, TM))],
        out_specs=(out_tok(d), out_tok(LANES), out_tok(LANES), _full((8, LANES))),
        scratch_shapes=[pltpu.VMEM((1, LANES), F32)],
        compiler_params=_cp("arbitrary", "arbitrary"),
        name="moe_route",
    )(h, mods, g.reshape(1, d), rpad, tri)


def _row_copy(src, i, dst, j, sem):
    return pltpu.make_async_copy(src.at[pl.ds(i, 1)], dst.at[pl.ds(j, 1)], sem)


def _dispatch_kernel(pos_ref, pad_ref, nv_ref, a_hbm, zero_hbm, xs_hbm, sems, *, tokens, chunk):
    n_chunks = tokens // chunk

    def wait_chunk(slot):
        pltpu.make_async_copy(a_hbm.at[pl.ds(0, 2 * chunk)], xs_hbm.at[pl.ds(0, 2 * chunk)],
                              sems.at[slot]).wait()

    def issue_chunk(c, carry):
        slot = lax.rem(c, 2)

        def issue(r, inner):
            t = c * chunk + r
            _row_copy(a_hbm, t, xs_hbm, pos_ref[2 * t], sems.at[slot]).start()
            _row_copy(a_hbm, t, xs_hbm, pos_ref[2 * t + 1], sems.at[slot]).start()
            return inner

        lax.fori_loop(0, chunk, issue, 0, unroll=8)

        @pl.when(c > 0)
        def _():
            wait_chunk(1 - slot)
        return carry

    lax.fori_loop(0, n_chunks, issue_chunk, 0)
    wait_chunk((n_chunks - 1) % 2)

    n_pad = pad_ref.shape[0]

    def pad_start(i, carry):
        @pl.when(pad_ref[i] >= 0)
        def _():
            _row_copy(zero_hbm, 0, xs_hbm, pad_ref[i], sems.at[0]).start()
        return carry

    def pad_wait(i, carry):
        @pl.when(pad_ref[i] >= 0)
        def _():
            _row_copy(zero_hbm, 0, xs_hbm, 0, sems.at[0]).wait()
        return carry

    lax.fori_loop(0, n_pad, pad_start, 0)
    lax.fori_loop(0, n_pad, pad_wait, 0)

    n_tiles = xs_hbm.shape[0] // chunk

    def tail_copy(i):
        row0 = pl.multiple_of((nv_ref[0] + i) * chunk, chunk)
        return pltpu.make_async_copy(zero_hbm, xs_hbm.at[pl.ds(row0, chunk)], sems.at[1])

    def tail_start(i, carry):
        @pl.when(nv_ref[0] + i < n_tiles)
        def _():
            tail_copy(i).start()
        return carry

    def tail_wait(i, carry):
        @pl.when(nv_ref[0] + i < n_tiles)
        def _():
            tail_copy(i).wait()
        return carry

    lax.fori_loop(0, n_tiles - (TOP_K * tokens) // chunk, tail_start, 0)
    lax.fori_loop(0, n_tiles - (TOP_K * tokens) // chunk, tail_wait, 0)


def _moe_dispatch(a2, pos, pad_rows, n_valid, n_rows):
    tokens, d = a2.shape
    kern = functools.partial(_dispatch_kernel, tokens=tokens, chunk=TM)
    any_spec = pl.BlockSpec(memory_space=pl.ANY)
    return pl.pallas_call(
        kern,
        out_shape=jax.ShapeDtypeStruct((n_rows, d), F32),
        grid_spec=pltpu.PrefetchScalarGridSpec(
            num_scalar_prefetch=3, grid=(1,),
            in_specs=[any_spec, any_spec], out_specs=any_spec,
            scratch_shapes=[pltpu.SemaphoreType.DMA((2,))]),
        compiler_params=pltpu.CompilerParams(dimension_semantics=("arbitrary",)),
        name="moe_dispatch",
    )(pos, pad_rows, n_valid, a2, jnp.zeros((TM, d), F32))


def _experts_kernel(te_ref, xi_ref, nv_ref, x_ref, w1_ref, w3_ref, w2_ref, y_ref, hm_ref):
    @pl.when(pl.program_id(0) < nv_ref[0])
    def _():
        _swiglu_hidden(x_ref[...].astype(BF16), w1_ref.at[0], w3_ref.at[0], hm_ref)
        y_ref[...] = _dot(hm_ref[...], w2_ref[0])

    @pl.when(pl.program_id(0) >= nv_ref[0])
    def _():
        y_ref[...] = jnp.zeros_like(y_ref)


def _moe_experts(xs, tile_expert, x_index, n_valid, w1, w3, w2, n_tiles):
    d = xs.shape[-1]
    f = w1.shape[-1]
    rows = pl.BlockSpec((TM, d), lambda i, te, xi, nv: (xi[i], 0))
    wspec = lambda s: pl.BlockSpec((1,) + s, lambda i, te, xi, nv: (te[i], 0, 0))
    return pl.pallas_call(
        _experts_kernel,
        out_shape=jax.ShapeDtypeStruct((n_tiles * TM, d), F32),
        grid_spec=pltpu.PrefetchScalarGridSpec(
            num_scalar_prefetch=3, grid=(n_tiles,),
            in_specs=[rows, wspec((d, f)), wspec((d, f)), wspec((f, d))],
            out_specs=pl.BlockSpec((TM, d), lambda i, te, xi, nv: (i, 0)),
            scratch_shapes=[pltpu.VMEM((TM, f), BF16)]),
        compiler_params=_cp("arbitrary"),
        name="moe_experts",
    )(tile_expert, x_index, n_valid, xs, w1.astype(BF16), w3.astype(BF16), w2.astype(BF16))


def _combine_kernel(pos_ref, h_ref, mod_ref, rw_ref, ys_hbm, o_ref, y1_ref, y2_ref, sems, *, nt):
    base = (pl.program_id(0) * nt + pl.program_id(1)) * TM

    def issue(r, carry):
        t = base + r
        _row_copy(ys_hbm, pos_ref[2 * t], y1_ref, r, sems.at[0]).start()
        _row_copy(ys_hbm, pos_ref[2 * t + 1], y2_ref, r, sems.at[1]).start()
        return carry

    lax.fori_loop(0, TM, issue, 0, unroll=8)
    pltpu.make_async_copy(ys_hbm.at[pl.ds(0, TM)], y1_ref, sems.at[0]).wait()
    pltpu.make_async_copy(ys_hbm.at[pl.ds(0, TM)], y2_ref, sems.at[1]).wait()
    rw = rw_ref[0]
    y = rw[:, 0:1] * y1_ref[...] + rw[:, 1:2] * y2_ref[...]
    o_ref[0] = h_ref[0] + mod_ref[0][5:6] * y


def _moe_combine(ys, pos, rw, h, mods, ctx_tiles, t0):
    nb, l, d = h.shape
    nt = l // TM - t0
    tok = lambda width, off: pl.BlockSpec((1, TM, width), lambda b, t, p: (b, t + off, 0))
    mod = pl.BlockSpec((1, 6, d), lambda b, t, p: (jnp.where(t + t0 < ctx_tiles, nb, b), 0, 0))
    return pl.pallas_call(
        functools.partial(_combine_kernel, nt=nt),
        out_shape=jax.ShapeDtypeStruct((nb, l, d), F32),
        grid_spec=pltpu.PrefetchScalarGridSpec(
            num_scalar_prefetch=1, grid=(nb, nt),
            in_specs=[tok(d, t0), mod, tok(LANES, 0), pl.BlockSpec(memory_space=pl.ANY)],
            out_specs=tok(d, t0),
            scratch_shapes=[pltpu.VMEM((TM, d), F32), pltpu.VMEM((TM, d), F32),
                            pltpu.SemaphoreType.DMA((2,))]),
        input_output_aliases={1: 0},
        compiler_params=_cp("arbitrary", "arbitrary"),
        name="moe_combine",
    )(pos, h, mods, rw, ys)


def _moe(h, mods, g, router, w1, w3, w2, ctx_tiles, t0):
    n_experts = router.shape[-1]
    a, ri, rw, counts = _moe_route(h, mods, g, router, ctx_tiles, t0)
    nb, n, d = a.shape
    tokens = nb * n
    n_tiles = (TOP_K * tokens) // TM + n_experts
    counts = counts[0, :n_experts]
    padded = ((counts + TM - 1) // TM) * TM
    ends = jnp.cumsum(padded)
    starts = ends - padded
    ri = ri.reshape(tokens, LANES)
    pos = (starts[ri[:, 0:TOP_K]] + ri[:, TOP_K:2 * TOP_K]).reshape(TOP_K * tokens)
    n_valid = ends[-1] // TM
    x_index = jnp.minimum(jnp.arange(n_tiles, dtype=jnp.int32), n_valid - 1)
    tile_expert = jnp.minimum(jnp.searchsorted(ends, x_index * TM, side="right"), n_experts - 1)
    fill = jnp.arange(TM, dtype=jnp.int32)[None, :]
    pad_rows = jnp.where(fill < (padded - counts)[:, None], (starts + counts)[:, None] + fill, -1)
    n_valid = n_valid.reshape(1).astype(jnp.int32)
    xs = _moe_dispatch(a.reshape(tokens, d), pos.astype(jnp.int32),
                       pad_rows.reshape(-1).astype(jnp.int32), n_valid, n_tiles * TM)
    ys = _moe_experts(xs, tile_expert.astype(jnp.int32), x_index, n_valid, w1, w3, w2, n_tiles)
    return _moe_combine(ys, pos.astype(jnp.int32), rw, h, mods, ctx_tiles, t0)


def _final_kernel(h_ref, g_ref, o_ref):
    x = h_ref[0]
    ms = jnp.mean(x * x, axis=-1, keepdims=True)
    o_ref[0] = x * lax.rsqrt(ms + EPS) * g_ref[...]


def _final_norm(h, g, t0):
    nb, l, d = h.shape
    nt = l // TM - t0
    return pl.pallas_call(
        _final_kernel,
        out_shape=jax.ShapeDtypeStruct((nb, nt * TM, d), F32),
        grid=(nb, nt),
        in_specs=[pl.BlockSpec((1, TM, d), lambda b, t: (b, t + t0, 0)), _full((1, d))],
        out_specs=pl.BlockSpec((1, TM, d), lambda b, t: (b, t, 0)),
        compiler_params=_cp("parallel", "arbitrary"),
        name="final_norm",
    )(h, g.reshape(1, d))


def kernel(x, c, ctx, c_ctx, ada_w, ada_b, norm_mix_g, norm_ffn_g, final_g, conv_pw1_w, conv_pw1_b, conv_dw_w, conv_dw_b, conv_ln_g, conv_ln_b, conv_pw2_w, conv_pw2_b, rec_w_in, rec_lb_logits, rec_onorm_g, rec_w_o, ffn_w1, ffn_w3, ffn_w2, moe_router, moe_w1, moe_w3, moe_w2):
    nb, seq, d = x.shape
    n_ctx = ctx.shape[1]
    depth = ada_w.shape[0]
    ctx_tiles = n_ctx // TM
    assert n_ctx % TM == 0 and seq % TM == 0 and n_ctx % BLK == 0 and seq % (GRID_W * 8) == 0
    assert d % (2 * 256) == 0 and nb < 16

    cond = jnp.zeros((16, d), F32).at[:nb].set(c).at[nb].set(c_ctx)
    mods = _modulation(cond, ada_w, ada_b)
    sm = jax.nn.softmax(rec_lb_logits.astype(F32), axis=1)
    lower = jnp.cumsum(sm, axis=1) - sm[:, :1]

    h = jnp.concatenate([ctx, x], axis=1)
    for i in range(depth):
        last = i == depth - 1
        j = i // 2
        t0 = ctx_tiles if last else 0
        if i % 2 == 0:
            u = _pw1_glu(h, mods[i], norm_mix_g[i], conv_pw1_w[j], conv_pw1_b[j], ctx_tiles)
            v = _dwconv(u, conv_dw_w[j], conv_dw_b[j], n_ctx)
            h = _pw2_residual(v, h, mods[i], conv_ln_g[j], conv_ln_b[j], conv_pw2_w[j], conv_pw2_b[j],
                              ctx_tiles, t0)
            h = _ffn_dense(h, mods[i], norm_ffn_g[i], ffn_w1[j], ffn_w3[j], ffn_w2[j], ctx_tiles)
        else:
            p = _rec_in_proj(h, mods[i], norm_mix_g[i], rec_w_in[j], lower[:, i], ctx_tiles)
            o_f, o_b = _gla(p, n_ctx)
            h = _rec_readout(o_f, o_b, p, h, mods[i], rec_onorm_g[j], rec_w_o[j], ctx_tiles, t0)
            h = _moe(h, mods[i], norm_ffn_g[i], moe_router[j], moe_w1[j], moe_w3[j], moe_w2[j], ctx_tiles, t0)
    return _final_norm(h, final_g, ctx_tiles)
```

```python
import functools

import numpy as np
import jax
import jax.numpy as jnp
from jax import lax
from jax.experimental import pallas as pl
from jax.experimental.pallas import tpu as pltpu

EPS = 1e-6
GRID_W = 64
CONV_PAD = 16
HEAD_DIM = 128
BLK = 128
CHUNK = 16
TOP_K = 2
TM = 256
LANES = 128
VMEM_LIMIT = 56 * 1024 * 1024

F32 = jnp.float32
BF16 = jnp.bfloat16


def _cp(*sem):
    return pltpu.CompilerParams(dimension_semantics=sem, vmem_limit_bytes=VMEM_LIMIT)


def _dot(a, b):
    return jnp.dot(a, b, preferred_element_type=F32)


def _dot_nt(a, b):
    return lax.dot_general(a, b, (((1,), (1,)), ((), ())), preferred_element_type=F32)


def _dot_tn(a, b):
    return lax.dot_general(a, b, (((0,), (0,)), ((), ())), preferred_element_type=F32)


def _sigmoid(x):
    return 1.0 / (1.0 + jnp.exp(-x))


def _silu(x):
    return x * _sigmoid(x)


def _rms_mod(x, g, shift, scale):
    ms = jnp.mean(x * x, axis=-1, keepdims=True)
    return (x * lax.rsqrt(ms + EPS) * g) * (1.0 + scale) + shift


def _mod_kernel(cond_ref, w_ref, b_ref, o_ref):
    s = _silu(cond_ref[...])
    o_ref[0] = jnp.dot(s, w_ref[0], preferred_element_type=F32,
                       precision=lax.Precision.HIGHEST) + b_ref[0]


def _modulation(cond, ada_w, ada_b):
    depth, d, n = ada_w.shape
    tn = n // 4
    out = pl.pallas_call(
        _mod_kernel,
        out_shape=jax.ShapeDtypeStruct((depth, cond.shape[0], n), F32),
        grid=(depth, n // tn),
        in_specs=[
            pl.BlockSpec(cond.shape, lambda l, j: (0, 0)),
            pl.BlockSpec((1, d, tn), lambda l, j: (l, 0, j)),
            pl.BlockSpec((1, 1, tn), lambda l, j: (l, 0, j)),
        ],
        out_specs=pl.BlockSpec((1, cond.shape[0], tn), lambda l, j: (l, 0, j)),
        compiler_params=_cp("arbitrary", "arbitrary"),
        name="modulation",
    )(cond, ada_w, ada_b.reshape(depth, 1, n))
    return out.reshape(depth, cond.shape[0], 6, d)


def _row_specs(nb, d, t0, ctx_tiles):
    tok = lambda width: pl.BlockSpec((1, TM, width), lambda b, t: (b, t + t0, 0))
    mod = pl.BlockSpec((1, 6, d), lambda b, t: (jnp.where(t + t0 < ctx_tiles, nb, b), 0, 0))
    return tok, mod


def _full(shape):
    return pl.BlockSpec(shape, lambda b, t: (0,) * len(shape))


def _layer(shape, j):
    return pl.BlockSpec((None,) + shape, lambda b, t: (j,) + (0,) * len(shape))


def _pw1_kernel(x_ref, mod_ref, g_ref, w_ref, b_ref, u_ref):
    m = mod_ref[0]
    a = _rms_mod(x_ref[0], g_ref[...], m[0:1], m[1:2]).astype(BF16)
    d = u_ref.shape[-1]
    cw = 512
    for c in range(0, d, cw):
        a1 = _dot(a, w_ref[:, c:c + cw]) + b_ref[:, c:c + cw]
        a2 = _dot(a, w_ref[:, d + c:d + c + cw]) + b_ref[:, d + c:d + c + cw]
        u_ref[0, :, c:c + cw] = a1 * _sigmoid(a2)


def _pw1_glu(h, mods, g, w, j, b, ctx_tiles):
    nb, l, d = h.shape
    tok, mod = _row_specs(nb, d, 0, ctx_tiles)
    return pl.pallas_call(
        _pw1_kernel,
        out_shape=jax.ShapeDtypeStruct((nb, l, d), F32),
        grid=(nb, l // TM),
        in_specs=[tok(d), mod, _full((1, d)), _layer((d, 2 * d), j), _full((1, 2 * d))],
        out_specs=tok(d),
        compiler_params=_cp("parallel", "arbitrary"),
        name="conv_pw1_glu",
    )(h, mods, g.reshape(1, d), w, b.reshape(1, 2 * d))


def _conv_kernel(u_ref, w_ref, b_ref, v_ref, pad_ref, *, ctx, rows, width_tiles):
    kw = w_ref.shape[0]
    half = (kw - 1) // 2
    cw = u_ref.shape[-1]
    j = pl.program_id(1)
    bias = jnp.broadcast_to(b_ref[...], (GRID_W, cw))
    zeros = jnp.zeros((CONV_PAD, cw), F32)

    def line_taps(win):
        acc = bias
        for k in range(kw):
            s = CONV_PAD + k - half
            acc = acc + w_ref[k:k + 1, :] * win[s:s + GRID_W]
        return acc

    pad_ref[0:CONV_PAD] = zeros
    pad_ref[CONV_PAD:CONV_PAD + ctx] = u_ref[0, 0:ctx]
    pad_ref[CONV_PAD + ctx:2 * CONV_PAD + ctx] = zeros
    for c0 in range(0, ctx, GRID_W):
        v_ref[0, c0:c0 + GRID_W] = line_taps(pad_ref[c0:c0 + GRID_W + 2 * CONV_PAD])

    @pl.when(j < width_tiles)
    def _along_width():
        stride = GRID_W + CONV_PAD
        pad_ref[0:CONV_PAD] = zeros
        for r in range(rows):
            base = CONV_PAD + r * stride
            pad_ref[base:base + GRID_W] = u_ref[0, ctx + r * GRID_W:ctx + (r + 1) * GRID_W]
            pad_ref[base + GRID_W:base + stride] = zeros

        def body(r, carry):
            base = pl.multiple_of(r * stride, 8)
            win = pad_ref[pl.ds(base, GRID_W + 2 * CONV_PAD)]
            out0 = pl.multiple_of(ctx + r * GRID_W, 8)
            v_ref[0, pl.ds(out0, GRID_W)] = line_taps(win)
            return carry

        lax.fori_loop(0, rows, body, 0)

    @pl.when(j >= width_tiles)
    def _along_height():
        npad = half * GRID_W
        zrow = jnp.zeros((GRID_W, cw), F32)
        for r in range(half):
            pad_ref[r * GRID_W:(r + 1) * GRID_W] = zrow
            lo = npad + (rows + r) * GRID_W
            pad_ref[lo:lo + GRID_W] = zrow
        pad_ref[npad:npad + rows * GRID_W] = u_ref[0, ctx:ctx + rows * GRID_W]

        def body(r, carry):
            acc = bias
            for k in range(kw):
                src = pl.multiple_of((r + k) * GRID_W, 8)
                acc = acc + w_ref[k:k + 1, :] * pad_ref[pl.ds(src, GRID_W)]
            out0 = pl.multiple_of(ctx + r * GRID_W, 8)
            v_ref[0, pl.ds(out0, GRID_W)] = acc
            return carry

        lax.fori_loop(0, rows, body, 0)


def _dwconv(u, w, b, ctx):
    nb, l, d = u.shape
    kw = w.shape[0]
    cw = 256
    rows = (l - ctx) // GRID_W
    half = (kw - 1) // 2
    pad_rows = max(2 * CONV_PAD + ctx, CONV_PAD + rows * (GRID_W + CONV_PAD),
                   (rows + 2 * half) * GRID_W)
    kern = functools.partial(_conv_kernel, ctx=ctx, rows=rows, width_tiles=(d // 2) // cw)
    return pl.pallas_call(
        kern,
        out_shape=jax.ShapeDtypeStruct((nb, l, d), F32),
        grid=(nb, d // cw),
        in_specs=[
            pl.BlockSpec((1, l, cw), lambda b_, j: (b_, 0, j)),
            pl.BlockSpec((kw, cw), lambda b_, j: (0, j)),
            pl.BlockSpec((1, cw), lambda b_, j: (0, j)),
        ],
        out_specs=pl.BlockSpec((1, l, cw), lambda b_, j: (b_, 0, j)),
        scratch_shapes=[pltpu.VMEM((pad_rows, cw), F32)],
        compiler_params=_cp("parallel", "arbitrary"),
        name="axial_dwconv",
    )(u, w, b.reshape(1, d))


def _pw2_kernel(v_ref, h_ref, mod_ref, lng_ref, lnb_ref, w_ref, b_ref, o_ref):
    v = v_ref[0]
    mu = jnp.mean(v, axis=-1, keepdims=True)
    vc = v - mu
    var = jnp.mean(vc * vc, axis=-1, keepdims=True)
    y = _silu(vc * lax.rsqrt(var + EPS) * lng_ref[...] + lnb_ref[...]).astype(BF16)
    out = _dot(y, w_ref[...]) + b_ref[...]
    o_ref[0] = h_ref[0] + mod_ref[0][2:3] * out


def _pw2_residual(v, h, mods, ln_g, ln_b, w, j, b, ctx_tiles, t0):
    nb, l, d = h.shape
    tok, mod = _row_specs(nb, d, t0, ctx_tiles)
    return pl.pallas_call(
        _pw2_kernel,
        out_shape=jax.ShapeDtypeStruct((nb, l, d), F32),
        grid=(nb, l // TM - t0),
        in_specs=[tok(d), tok(d), mod, _full((1, d)), _full((1, d)), _layer((d, d), j), _full((1, d))],
        out_specs=tok(d),
        input_output_aliases={1: 0},
        compiler_params=_cp("parallel", "arbitrary"),
        name="conv_pw2_residual",
    )(v, h, mods, ln_g.reshape(1, d), ln_b.reshape(1, d), w, b.reshape(1, d))


def _swiglu_hidden(a, w1_ref, w3_ref, hm_ref):
    f = hm_ref.shape[-1]
    cw = 256
    for c in range(0, f, cw):
        h1 = _dot(a, w1_ref[:, c:c + cw])
        h3 = _dot(a, w3_ref[:, c:c + cw])
        hm_ref[:, c:c + cw] = (_silu(h1) * h3).astype(BF16)


def _ffn_kernel(h_ref, mod_ref, g_ref, w1_ref, w3_ref, w2_ref, o_ref, hm_ref):
    m = mod_ref[0]
    h = h_ref[0]
    a = _rms_mod(h, g_ref[...], m[3:4], m[4:5]).astype(BF16)
    _swiglu_hidden(a, w1_ref, w3_ref, hm_ref)
    o_ref[0] = h + m[5:6] * _dot(hm_ref[...], w2_ref[...])


def _ffn_dense(h, mods, g, w1, w3, w2, j, ctx_tiles):
    nb, l, d = h.shape
    f = w1.shape[-1]
    tok, mod = _row_specs(nb, d, 0, ctx_tiles)
    return pl.pallas_call(
        _ffn_kernel,
        out_shape=jax.ShapeDtypeStruct((nb, l, d), F32),
        grid=(nb, l // TM),
        in_specs=[tok(d), mod, _full((1, d)), _layer((d, f), j), _layer((d, f), j), _layer((f, d), j)],
        out_specs=tok(d),
        scratch_shapes=[pltpu.VMEM((TM, f), BF16)],
        input_output_aliases={0: 0},
        compiler_params=_cp("parallel", "arbitrary"),
        name="ffn_swiglu",
    )(h, mods, g.reshape(1, d), w1, w3, w2)


def _win_kernel(x_ref, mod_ref, g_ref, w_ref, lb_ref, p_ref):
    m = mod_ref[0]
    a = _rms_mod(x_ref[0], g_ref[...], m[0:1], m[1:2]).astype(BF16)
    d = x_ref.shape[-1]
    cw = 512
    for c in range(0, 5 * d, cw):
        part, off = divmod(c, d)
        y = _dot(a, w_ref[:, c:c + cw])
        if part in (1, 2):
            lb = lb_ref[part - 1:part, off:off + cw]
            y = jnp.log(lb + (1.0 - lb) * _sigmoid(y))
        elif part == 3:
            y = _silu(y) * (HEAD_DIM ** -0.5)
        elif part == 4:
            y = _silu(y)
        p_ref[0, :, c:c + cw] = y


def _rec_in_proj(h, mods, g, w, j, lower, ctx_tiles):
    nb, l, d = h.shape
    n = w.shape[-1]
    tok, mod = _row_specs(nb, d, 0, ctx_tiles)
    return pl.pallas_call(
        _win_kernel,
        out_shape=jax.ShapeDtypeStruct((nb, l, n), F32),
        grid=(nb, l // TM),
        in_specs=[tok(d), mod, _full((1, d)), _layer((d, n), j), _full((2, d))],
        out_specs=tok(n),
        compiler_params=_cp("parallel", "arbitrary"),
        name="rec_in_proj",
    )(h, mods, g.reshape(1, d), w, lower)


N_CHUNKS = BLK // CHUNK
N_CHUNK_SUMS = 7
SUM_ROWS = BLK + N_CHUNK_SUMS * N_CHUNKS * 8 + 8


def _rec_constants():
    t = np.arange(BLK)
    mats, masks = [], []
    for direction in range(2):
        p = t if direction == 0 else BLK - 1 - t
        pt, ps = p[:, None], p[None, :]
        ct, cs = pt // CHUNK, ps // CHUNK
        same = ct == cs
        mid4 = N_CHUNKS // 2
        mid2 = (ct // 4) * 4 + 2
        chunk_level = [
            same,
            cs < ct,
            cs > ct,
            (ct >= mid4) & (cs >= mid4) & (cs < ct),
            (ct < mid4) & (cs > ct) & (cs < mid4),
            (ct >= mid2) & (cs >= mid2) & (cs < ct),
            (ct < mid2) & (cs > ct) & (cs < mid2),
        ]
        rows = [same & (ps <= pt)]
        for m in chunk_level:
            m = np.broadcast_to(m, (BLK, BLK))
            rows.append(np.repeat(m[::CHUNK], 8, axis=0))
        rows.append(np.ones((8, BLK), bool))
        mats.append(np.concatenate([r.astype(np.float32) for r in rows], axis=0))
        m = [
            same & (ps <= pt),
            (ct % 2 == 1) & (cs == ct - 1),
            (ct // 4 == cs // 4) & (ct % 4 >= 2) & (cs % 4 < 2),
            (ct >= mid4) & (cs < mid4),
        ]
        masks.append(np.stack([np.broadcast_to(x, (BLK, BLK)).astype(np.float32) for x in m], axis=0))
    return np.stack(mats), np.stack(masks)


def _chunk_scale(x, ec, i):
    base = i * N_CHUNKS * 8
    parts = []
    for j in range(N_CHUNKS):
        e = ec[base + 8 * j:base + 8 * j + 8]
        parts.append(x[CHUNK * j:CHUNK * j + 8] * e)
        parts.append(x[CHUNK * j + 8:CHUNK * j + 16] * e)
    return jnp.concatenate(parts, axis=0)


def _gla_kernel(v_ref, lff_ref, lfb_ref, q_ref, a_ref, m_ref, of_ref, ob_ref, *, ctx_blocks, n_blocks):
    last = n_blocks - 1
    n_heads = v_ref.shape[-1] // HEAD_DIM
    chains = [(dr, hd) for dr in range(2) for hd in range(n_heads)]
    lf_refs = (lff_ref, lfb_ref)
    o_refs = (of_ref, ob_ref)

    def step(n, states):
        rev = jnp.where(n < ctx_blocks, ctx_blocks - 1 - n, last + ctx_blocks - n)
        row0 = (pl.multiple_of(n * BLK, BLK), pl.multiple_of(rev * BLK, BLK))

        sums = []
        for dr in range(2):
            logf = lf_refs[dr][0, pl.ds(row0[dr], BLK), :]
            hi = logf.astype(BF16)
            lo = (logf - hi.astype(F32)).astype(BF16)
            sums.append((logf, _dot(a_ref[dr], jnp.concatenate([hi, lo], axis=0))))

        ops = []
        for dr, hd in chains:
            cols = slice(hd * HEAD_DIM, (hd + 1) * HEAD_DIM)
            logf, s = sums[dr]
            s = s[:, cols]
            k = 1.0 - jnp.exp(logf[:, cols])
            qs = q_ref[0, pl.ds(row0[dr], BLK), cols]
            bcum = s[0:BLK]
            ec = jnp.exp(s[BLK:SUM_ROWS])
            q_in = qs * jnp.exp(bcum)
            k_in = k * jnp.exp(-bcum)
            k_st = _chunk_scale(k_in, ec, 0)
            ops.append(dict(
                q_blk=_chunk_scale(q_in, ec, 1).astype(BF16),
                k_end=_chunk_scale(k_st, ec, 2).astype(BF16),
                q4=_chunk_scale(q_in, ec, 3).astype(BF16),
                k4=_chunk_scale(k_st, ec, 4).astype(BF16),
                q2=_chunk_scale(q_in, ec, 5).astype(BF16),
                k2=_chunk_scale(k_st, ec, 6).astype(BF16),
                e_tot=ec[N_CHUNK_SUMS * N_CHUNKS * 8:N_CHUNK_SUMS * N_CHUNKS * 8 + 1],
                q_in=q_in.astype(BF16), k_in=k_in.astype(BF16), k_st=k_st.astype(BF16),
                vb=v_ref[0, pl.ds(row0[dr], BLK), cols].astype(BF16)))

        raw = [(_dot_nt(o["q_in"], o["k_in"]), _dot_nt(o["q_in"], o["k_st"]),
                _dot_nt(o["q2"], o["k2"]), _dot_nt(o["q4"], o["k4"]),
                _dot_nt(o["q_blk"], st.astype(BF16)), _dot_tn(o["vb"], o["k_end"]))
               for o, st in zip(ops, states)]

        atts = []
        for (dr, hd), r in zip(chains, raw):
            att = jnp.where(m_ref[dr, 0] > 0.5, r[0], 0.0)
            att = att + jnp.where(m_ref[dr, 1] > 0.5, r[1], 0.0)
            att = att + jnp.where(m_ref[dr, 2] > 0.5, r[2], 0.0)
            att = att + jnp.where(m_ref[dr, 3] > 0.5, r[3], 0.0)
            atts.append(att.astype(BF16))

        new_states = []
        for (dr, hd), o, r, att, st in zip(chains, ops, raw, atts, states):
            cols = slice(hd * HEAD_DIM, (hd + 1) * HEAD_DIM)
            o_refs[dr][0, pl.ds(row0[dr], BLK), cols] = _dot(att, o["vb"]) + r[4]
            new_states.append(o["e_tot"] * st + r[5])
        return tuple(new_states)

    zero = jnp.zeros((HEAD_DIM, HEAD_DIM), F32)
    lax.fori_loop(0, n_blocks, step, (zero,) * len(chains))


GLA_HEADS_PER_STEP = 2


def _gla(p, ctx):
    nb, l, n5 = p.shape
    d = n5 // 5
    cw = GLA_HEADS_PER_STEP * HEAD_DIM
    groups = d // cw
    amat, masks = _rec_constants()
    amat = np.concatenate([amat, amat], axis=2)
    kern = functools.partial(_gla_kernel, ctx_blocks=ctx // BLK, n_blocks=l // BLK)
    col = lambda base: pl.BlockSpec((1, l, cw), lambda b, h: (b, 0, base + h))
    out = jax.ShapeDtypeStruct((nb, l, d), F32)
    return pl.pallas_call(
        kern,
        out_shape=(out, out),
        grid=(nb, groups),
        in_specs=[col(0), col(groups), col(2 * groups), col(3 * groups),
                  _full(amat.shape), _full(masks.shape)],
        out_specs=(col(0), col(0)),
        compiler_params=_cp("parallel", "arbitrary"),
        name="hgrn2_gla",
    )(p, p, p, p, jnp.asarray(amat, BF16), jnp.asarray(masks, F32))


def _readout_kernel(of_ref, ob_ref, g_ref, h_ref, mod_ref, ng_ref, w_ref, out_ref, y_ref):
    d = of_ref.shape[-1]
    for c in range(0, d, HEAD_DIM):
        o = of_ref[0, :, c:c + HEAD_DIM] + ob_ref[0, :, c:c + HEAD_DIM]
        o = o * lax.rsqrt(jnp.mean(o * o, axis=-1, keepdims=True) + EPS) * ng_ref[:, c:c + HEAD_DIM]
        y_ref[:, c:c + HEAD_DIM] = (o * g_ref[0, :, c:c + HEAD_DIM]).astype(BF16)
    out_ref[0] = h_ref[0] + mod_ref[0][2:3] * _dot(y_ref[...], w_ref[...])


def _rec_readout(o_f, o_b, p, h, mods, onorm_g, w_o, j, ctx_tiles, t0):
    nb, l, d = h.shape
    tok, mod = _row_specs(nb, d, t0, ctx_tiles)
    gate_cols = pl.BlockSpec((1, TM, d), lambda b, t: (b, t + t0, 4))
    return pl.pallas_call(
        _readout_kernel,
        out_shape=jax.ShapeDtypeStruct((nb, l, d), F32),
        grid=(nb, l // TM - t0),
        in_specs=[tok(d), tok(d), gate_cols, tok(d), mod, _full((1, d)), _layer((d, d), j)],
        out_specs=tok(d),
        scratch_shapes=[pltpu.VMEM((TM, d), BF16)],
        input_output_aliases={3: 0},
        compiler_params=_cp("parallel", "arbitrary"),
        name="rec_readout",
    )(o_f, o_b, p, h, mods, onorm_g.reshape(1, d), w_o)


def _route_kernel(h_ref, mod_ref, g_ref, r_ref, tri_ref, a_ref, ri_ref, rw_ref, cnt_ref, carry_ref,
                  *, n_experts):
    @pl.when((pl.program_id(0) == 0) & (pl.program_id(1) == 0))
    def _():
        carry_ref[...] = jnp.zeros_like(carry_ref)

    m = mod_ref[0]
    a = _rms_mod(h_ref[0], g_ref[...], m[3:4], m[4:5])
    a_ref[0] = a
    logits = jnp.dot(a, r_ref[...], preferred_element_type=F32, precision=lax.Precision.HIGHEST)
    lane = lax.broadcasted_iota(jnp.int32, logits.shape, 1)
    neg = jnp.float32(-jnp.inf)
    logits = jnp.where(lane < n_experts, logits, neg)
    v1 = jnp.max(logits, axis=-1, keepdims=True)
    i1 = jnp.min(jnp.where(logits == v1, lane, LANES), axis=-1, keepdims=True)
    rest = jnp.where(lane == i1, neg, logits)
    v2 = jnp.max(rest, axis=-1, keepdims=True)
    i2 = jnp.min(jnp.where(rest == v2, lane, LANES), axis=-1, keepdims=True)
    e2 = jnp.exp(v2 - v1)
    w1 = 1.0 / (1.0 + e2)
    w2 = e2 / (1.0 + e2)
    oh1 = jnp.where(lane == i1, 1.0, 0.0)
    oh2 = jnp.where(lane == i2, 1.0, 0.0)
    both = oh1 + oh2
    earlier = _dot(tri_ref[...], both.astype(BF16)) + carry_ref[...]
    r1 = jnp.sum(oh1 * earlier, axis=-1, keepdims=True).astype(jnp.int32)
    r2 = jnp.sum(oh2 * earlier, axis=-1, keepdims=True).astype(jnp.int32)
    carry = carry_ref[...] + jnp.sum(both, axis=0, keepdims=True)
    carry_ref[...] = carry
    cnt_ref[...] = jnp.broadcast_to(carry, cnt_ref.shape).astype(jnp.int32)
    ri_ref[0] = jnp.where(lane == 0, i1, jnp.where(lane == 1, i2, jnp.where(lane == 2, r1,
                          jnp.where(lane == 3, r2, 0))))
    rw_ref[0] = jnp.where(lane == 0, w1, jnp.where(lane == 1, w2, 0.0))


def _moe_route(h, mods, g, router, ctx_tiles, t0):
    nb, l, d = h.shape
    n_experts = router.shape[-1]
    nt = l // TM - t0
    tok, mod = _row_specs(nb, d, t0, ctx_tiles)
    out_tok = lambda width: pl.BlockSpec((1, TM, width), lambda b, t: (b, t, 0))
    rpad = jnp.pad(router, ((0, 0), (0, LANES - n_experts)))
    tri = jnp.asarray(np.tril(np.ones((TM, TM), np.float32), -1), BF16)
    return pl.pallas_call(
        functools.partial(_route_kernel, n_experts=n_experts),
        out_shape=(jax.ShapeDtypeStruct((nb, nt * TM, d), F32),
                   jax.ShapeDtypeStruct((nb, nt * TM, LANES), jnp.int32),
                   jax.ShapeDtypeStruct((nb, nt * TM, LANES), F32),
                   jax.ShapeDtypeStruct((8, LANES), jnp.int32)),
        grid=(nb, nt),
        in_specs=[tok(d), mod, _full((1, d)), _full((d, LANES)), _full((TM, TM))],
        out_specs=(out_tok(d), out_tok(LANES), out_tok(LANES), _full((8, LANES))),
        scratch_shapes=[pltpu.VMEM((1, LANES), F32)],
        compiler_params=_cp("arbitrary", "arbitrary"),
        name="moe_route",
    )(h, mods, g.reshape(1, d), rpad, tri)


def _row_copy(src, i, dst, j, sem):
    return pltpu.make_async_copy(src.at[pl.ds(i, 1)], dst.at[pl.ds(j, 1)], sem)


def _dispatch_kernel(pos_ref, pad_ref, nv_ref, a_ref, xs_hbm, zero_ref, sems, *, n_tok_tiles):
    step = pl.program_id(0)
    base = step * TM
    tile = a_ref.at[0]

    def issue(r, carry):
        t = base + r
        _row_copy(tile, r, xs_hbm, pos_ref[2 * t], sems.at[0]).start()
        _row_copy(tile, r, xs_hbm, pos_ref[2 * t + 1], sems.at[0]).start()
        return carry

    lax.fori_loop(0, TM, issue, 0, unroll=8)
    for _ in range(TOP_K):
        pltpu.make_async_copy(tile, xs_hbm.at[pl.ds(0, TM)], sems.at[0]).wait()

    @pl.when(step == n_tok_tiles - 1)
    def _fill():
        zero_ref[...] = jnp.zeros_like(zero_ref)

        def pad_start(i, carry):
            @pl.when(pad_ref[i] >= 0)
            def _():
                _row_copy(zero_ref, 0, xs_hbm, pad_ref[i], sems.at[1]).start()
            return carry

        def pad_wait(i, carry):
            @pl.when(pad_ref[i] >= 0)
            def _():
                _row_copy(zero_ref, 0, xs_hbm, 0, sems.at[1]).wait()
            return carry

        lax.fori_loop(0, pad_ref.shape[0], pad_start, 0)
        lax.fori_loop(0, pad_ref.shape[0], pad_wait, 0)

        n_tiles = xs_hbm.shape[0] // TM
        n_tail = n_tiles - TOP_K * n_tok_tiles

        def tail_copy(i):
            row0 = pl.multiple_of((nv_ref[0] + i) * TM, TM)
            return pltpu.make_async_copy(zero_ref, xs_hbm.at[pl.ds(row0, TM)], sems.at[1])

        def tail_start(i, carry):
            @pl.when(nv_ref[0] + i < n_tiles)
            def _():
                tail_copy(i).start()
            return carry

        def tail_wait(i, carry):
            @pl.when(nv_ref[0] + i < n_tiles)
            def _():
                tail_copy(i).wait()
            return carry

        lax.fori_loop(0, n_tail, tail_start, 0)
        lax.fori_loop(0, n_tail, tail_wait, 0)


def _moe_dispatch(a, pos, pad_rows, n_valid, n_rows):
    nb, n, d = a.shape
    nt = n // TM
    kern = functools.partial(_dispatch_kernel, n_tok_tiles=nb * nt)
    return pl.pallas_call(
        kern,
        out_shape=jax.ShapeDtypeStruct((n_rows, d), F32),
        grid_spec=pltpu.PrefetchScalarGridSpec(
            num_scalar_prefetch=3, grid=(nb * nt,),
            in_specs=[pl.BlockSpec((1, TM, d), lambda i, p, q, v: (i // nt, i % nt, 0))],
            out_specs=pl.BlockSpec(memory_space=pl.ANY),
            scratch_shapes=[pltpu.VMEM((TM, d), F32), pltpu.SemaphoreType.DMA((2,))]),
        compiler_params=_cp("arbitrary"),
        name="moe_dispatch",
    )(pos, pad_rows, n_valid, a)


def _experts_kernel(te_ref, xi_ref, nv_ref, x_ref, w1_ref, w3_ref, w2_ref, y_ref, hm_ref):
    @pl.when(pl.program_id(0) < nv_ref[0])
    def _():
        _swiglu_hidden(x_ref[...].astype(BF16), w1_ref.at[0], w3_ref.at[0], hm_ref)
        y_ref[...] = _dot(hm_ref[...], w2_ref[0])

    @pl.when(pl.program_id(0) >= nv_ref[0])
    def _():
        y_ref[...] = jnp.zeros_like(y_ref)


def _moe_experts(xs, tile_expert, x_index, n_valid, w1, w3, w2, j, n_tiles):
    d = xs.shape[-1]
    f = w1.shape[-1]
    rows = pl.BlockSpec((TM, d), lambda i, te, xi, nv: (xi[i], 0))
    wspec = lambda s: pl.BlockSpec((None, 1) + s, lambda i, te, xi, nv: (j, te[i], 0, 0))
    return pl.pallas_call(
        _experts_kernel,
        out_shape=jax.ShapeDtypeStruct((n_tiles * TM, d), F32),
        grid_spec=pltpu.PrefetchScalarGridSpec(
            num_scalar_prefetch=3, grid=(n_tiles,),
            in_specs=[rows, wspec((d, f)), wspec((d, f)), wspec((f, d))],
            out_specs=pl.BlockSpec((TM, d), lambda i, te, xi, nv: (i, 0)),
            scratch_shapes=[pltpu.VMEM((TM, f), BF16)]),
        compiler_params=_cp("arbitrary"),
        name="moe_experts",
    )(tile_expert, x_index, n_valid, xs, w1, w3, w2)


def _combine_kernel(pos_ref, h_ref, mod_ref, rw_ref, ys_hbm, o_ref, y1_ref, y2_ref, sems, *, nt, fg_ref=None):
    base = (pl.program_id(0) * nt + pl.program_id(1)) * TM

    def issue(r, carry):
        t = base + r
        _row_copy(ys_hbm, pos_ref[2 * t], y1_ref, r, sems.at[0]).start()
        _row_copy(ys_hbm, pos_ref[2 * t + 1], y2_ref, r, sems.at[1]).start()
        return carry

    lax.fori_loop(0, TM, issue, 0, unroll=8)
    pltpu.make_async_copy(ys_hbm.at[pl.ds(0, TM)], y1_ref, sems.at[0]).wait()
    pltpu.make_async_copy(ys_hbm.at[pl.ds(0, TM)], y2_ref, sems.at[1]).wait()
    rw = rw_ref[0]
    y = rw[:, 0:1] * y1_ref[...] + rw[:, 1:2] * y2_ref[...]
    out = h_ref[0] + mod_ref[0][5:6] * y
    if fg_ref is not None:
        out = out * lax.rsqrt(jnp.mean(out * out, axis=-1, keepdims=True) + EPS) * fg_ref[...]
    o_ref[0] = out


def _combine_final_kernel(pos_ref, h_ref, mod_ref, rw_ref, fg_ref, ys_hbm, o_ref, y1_ref, y2_ref, sems, *, nt):
    _combine_kernel(pos_ref, h_ref, mod_ref, rw_ref, ys_hbm, o_ref, y1_ref, y2_ref, sems, nt=nt, fg_ref=fg_ref)


def _moe_combine(ys, pos, rw, h, mods, ctx_tiles, t0, final_g=None):
    nb, l, d = h.shape
    nt = l // TM - t0
    tok = lambda width, off: pl.BlockSpec((1, TM, width), lambda b, t, p: (b, t + off, 0))
    mod = pl.BlockSpec((1, 6, d), lambda b, t, p: (jnp.where(t + t0 < ctx_tiles, nb, b), 0, 0))
    scratch = [pltpu.VMEM((TM, d), F32), pltpu.VMEM((TM, d), F32), pltpu.SemaphoreType.DMA((2,))]
    any_spec = pl.BlockSpec(memory_space=pl.ANY)
    if final_g is None:
        return pl.pallas_call(
            functools.partial(_combine_kernel, nt=nt),
            out_shape=jax.ShapeDtypeStruct((nb, l, d), F32),
            grid_spec=pltpu.PrefetchScalarGridSpec(
                num_scalar_prefetch=1, grid=(nb, nt),
                in_specs=[tok(d, t0), mod, tok(LANES, 0), any_spec],
                out_specs=tok(d, t0), scratch_shapes=scratch),
            input_output_aliases={1: 0},
            compiler_params=_cp("arbitrary", "arbitrary"),
            name="moe_combine",
        )(pos, h, mods, rw, ys)
    return pl.pallas_call(
        functools.partial(_combine_final_kernel, nt=nt),
        out_shape=jax.ShapeDtypeStruct((nb, nt * TM, d), F32),
        grid_spec=pltpu.PrefetchScalarGridSpec(
            num_scalar_prefetch=1, grid=(nb, nt),
            in_specs=[tok(d, t0), mod, tok(LANES, 0), pl.BlockSpec((1, d), lambda b, t, p: (0, 0)), any_spec],
            out_specs=tok(d, 0), scratch_shapes=scratch),
        compiler_params=_cp("arbitrary", "arbitrary"),
        name="moe_combine_final",
    )(pos, h, mods, rw, final_g.reshape(1, d), ys)


def _moe(h, mods, g, router, w1, w3, w2, j, ctx_tiles, t0, final_g=None):
    n_experts = router.shape[-1]
    a, ri, rw, counts = _moe_route(h, mods, g, router, ctx_tiles, t0)
    nb, n, d = a.shape
    tokens = nb * n
    n_tiles = (TOP_K * tokens) // TM + n_experts
    counts = counts[0, :n_experts]
    padded = ((counts + TM - 1) // TM) * TM
    ends = jnp.cumsum(padded)
    starts = ends - padded
    ri = ri.reshape(tokens, LANES)
    pos = (starts[ri[:, 0:TOP_K]] + ri[:, TOP_K:2 * TOP_K]).reshape(TOP_K * tokens)
    n_valid = ends[-1] // TM
    x_index = jnp.maximum(jnp.minimum(jnp.arange(n_tiles, dtype=jnp.int32), n_valid - 1), 0)
    tile_expert = jnp.minimum(jnp.sum(ends[None, :] <= (x_index * TM)[:, None], axis=1), n_experts - 1)
    fill = jnp.arange(TM, dtype=jnp.int32)[None, :]
    pad_rows = jnp.where(fill < (padded - counts)[:, None], (starts + counts)[:, None] + fill, -1)
    n_valid = n_valid.reshape(1).astype(jnp.int32)
    xs = _moe_dispatch(a, pos.astype(jnp.int32),
                       pad_rows.reshape(-1).astype(jnp.int32), n_valid, n_tiles * TM)
    ys = _moe_experts(xs, tile_expert.astype(jnp.int32), x_index, n_valid, w1, w3, w2, j, n_tiles)
    return _moe_combine(ys, pos.astype(jnp.int32), rw, h, mods, ctx_tiles, t0, final_g)


def kernel(x, c, ctx, c_ctx, ada_w, ada_b, norm_mix_g, norm_ffn_g, final_g, conv_pw1_w, conv_pw1_b, conv_dw_w, conv_dw_b, conv_ln_g, conv_ln_b, conv_pw2_w, conv_pw2_b, rec_w_in, rec_lb_logits, rec_onorm_g, rec_w_o, ffn_w1, ffn_w3, ffn_w2, moe_router, moe_w1, moe_w3, moe_w2):
    nb, seq, d = x.shape
    n_ctx = ctx.shape[1]
    depth = ada_w.shape[0]
    ctx_tiles = n_ctx // TM
    assert n_ctx % TM == 0 and seq % TM == 0 and n_ctx % BLK == 0 and seq % (GRID_W * 8) == 0
    assert d % (2 * 256) == 0 and nb < 16 and depth % 2 == 0

    cond = jnp.zeros((16, d), F32).at[:nb].set(c).at[nb].set(c_ctx)
    mods = _modulation(cond, ada_w, ada_b)
    sm = jax.nn.softmax(rec_lb_logits.astype(F32), axis=1)
    lower = jnp.cumsum(sm, axis=1) - sm[:, :1]

    bf = lambda w: w.astype(BF16)
    conv_pw1_w, conv_pw2_w, rec_w_in, rec_w_o = bf(conv_pw1_w), bf(conv_pw2_w), bf(rec_w_in), bf(rec_w_o)
    ffn_w1, ffn_w3, ffn_w2 = bf(ffn_w1), bf(ffn_w3), bf(ffn_w2)
    moe_w1, moe_w3, moe_w2 = bf(moe_w1), bf(moe_w3), bf(moe_w2)

    h = jnp.concatenate([ctx, x], axis=1)
    for i in range(depth):
        last = i == depth - 1
        j = i // 2
        t0 = ctx_tiles if last else 0
        if i % 2 == 0:
            u = _pw1_glu(h, mods[i], norm_mix_g[i], conv_pw1_w, j, conv_pw1_b[j], ctx_tiles)
            v = _dwconv(u, conv_dw_w[j], conv_dw_b[j], n_ctx)
            h = _pw2_residual(v, h, mods[i], conv_ln_g[j], conv_ln_b[j], conv_pw2_w, j, conv_pw2_b[j],
                              ctx_tiles, t0)
            h = _ffn_dense(h, mods[i], norm_ffn_g[i], ffn_w1, ffn_w3, ffn_w2, j, ctx_tiles)
        else:
            p = _rec_in_proj(h, mods[i], norm_mix_g[i], rec_w_in, j, lower[:, i], ctx_tiles)
            o_f, o_b = _gla(p, n_ctx)
            h = _rec_readout(o_f, o_b, p, h, mods[i], rec_onorm_g[j], rec_w_o, j, ctx_tiles, t0)
            h = _moe(h, mods[i], norm_ffn_g[i], moe_router[j], moe_w1, moe_w3, moe_w2, j, ctx_tiles, t0,
                     final_g if last else None)
    return h
```

```python
import functools

import numpy as np
import jax
import jax.numpy as jnp
from jax import lax
from jax.experimental import pallas as pl
from jax.experimental.pallas import tpu as pltpu

EPS = 1e-6
GRID_W = 64
CONV_PAD = 16
HEAD_DIM = 128
BLK = 128
CHUNK = 16
TOP_K = 2
TM = 256
LANES = 128
VMEM_LIMIT = 56 * 1024 * 1024

F32 = jnp.float32
BF16 = jnp.bfloat16


def _cp(*sem):
    return pltpu.CompilerParams(dimension_semantics=sem, vmem_limit_bytes=VMEM_LIMIT)


def _dot(a, b):
    return jnp.dot(a, b, preferred_element_type=F32)


def _dot_nt(a, b):
    return lax.dot_general(a, b, (((1,), (1,)), ((), ())), preferred_element_type=F32)


def _dot_tn(a, b):
    return lax.dot_general(a, b, (((0,), (0,)), ((), ())), preferred_element_type=F32)


def _sigmoid(x):
    return 1.0 / (1.0 + jnp.exp(-x))


def _silu(x):
    return x * _sigmoid(x)


def _rms_mod(x, g, shift, scale):
    ms = jnp.mean(x * x, axis=-1, keepdims=True)
    return (x * lax.rsqrt(ms + EPS) * g) * (1.0 + scale) + shift


def _mod_kernel(cond_ref, w_ref, b_ref, o_ref):
    s = _silu(cond_ref[...])
    o_ref[0] = jnp.dot(s, w_ref[0], preferred_element_type=F32,
                       precision=lax.Precision.HIGHEST) + b_ref[0]


def _modulation(cond, ada_w, ada_b):
    depth, d, n = ada_w.shape
    tn = n // 4
    out = pl.pallas_call(
        _mod_kernel,
        out_shape=jax.ShapeDtypeStruct((depth, cond.shape[0], n), F32),
        grid=(depth, n // tn),
        in_specs=[
            pl.BlockSpec(cond.shape, lambda l, j: (0, 0)),
            pl.BlockSpec((1, d, tn), lambda l, j: (l, 0, j)),
            pl.BlockSpec((1, 1, tn), lambda l, j: (l, 0, j)),
        ],
        out_specs=pl.BlockSpec((1, cond.shape[0], tn), lambda l, j: (l, 0, j)),
        compiler_params=_cp("arbitrary", "arbitrary"),
        name="modulation",
    )(cond, ada_w, ada_b.reshape(depth, 1, n))
    return out.reshape(depth, cond.shape[0], 6, d)


def _row_specs(nb, d, t0, ctx_tiles):
    tok = lambda width: pl.BlockSpec((1, TM, width), lambda b, t: (b, t + t0, 0))
    mod = pl.BlockSpec((1, 6, d), lambda b, t: (jnp.where(t + t0 < ctx_tiles, nb, b), 0, 0))
    return tok, mod


def _full(shape):
    return pl.BlockSpec(shape, lambda b, t: (0,) * len(shape))


def _layer(shape, j):
    return pl.BlockSpec((None,) + shape, lambda b, t: (j,) + (0,) * len(shape))


def _pw1_kernel(x_ref, mod_ref, g_ref, w_ref, b_ref, u_ref):
    m = mod_ref[0]
    a = _rms_mod(x_ref[0], g_ref[...], m[0:1], m[1:2]).astype(BF16)
    d = u_ref.shape[-1]
    cw = 512
    for c in range(0, d, cw):
        a1 = _dot(a, w_ref[:, c:c + cw]) + b_ref[:, c:c + cw]
        a2 = _dot(a, w_ref[:, d + c:d + c + cw]) + b_ref[:, d + c:d + c + cw]
        u_ref[0, :, c:c + cw] = a1 * _sigmoid(a2)


def _pw1_glu(h, mods, g, w, j, b, ctx_tiles):
    nb, l, d = h.shape
    tok, mod = _row_specs(nb, d, 0, ctx_tiles)
    return pl.pallas_call(
        _pw1_kernel,
        out_shape=jax.ShapeDtypeStruct((nb, l, d), F32),
        grid=(nb, l // TM),
        in_specs=[tok(d), mod, _full((1, d)), _layer((d, 2 * d), j), _full((1, 2 * d))],
        out_specs=tok(d),
        compiler_params=_cp("parallel", "arbitrary"),
        name="conv_pw1_glu",
    )(h, mods, g.reshape(1, d), w, b.reshape(1, 2 * d))


def _conv_kernel(u_ref, w_ref, b_ref, v_ref, pad_ref, *, ctx, rows, width_tiles):
    kw = w_ref.shape[0]
    half = (kw - 1) // 2
    cw = u_ref.shape[-1]
    j = pl.program_id(1)
    bias = jnp.broadcast_to(b_ref[...], (GRID_W, cw))
    zeros = jnp.zeros((CONV_PAD, cw), F32)

    def line_taps(win):
        acc = bias
        n = win.shape[0]
        shifted = {0: win}
        for k in range(kw):
            s = CONV_PAD + k - half
            b, a8 = s % 8, (s // 8) * 8
            if b not in shifted:
                shifted[b] = pltpu.roll(win, n - b, axis=0)
            acc = acc + w_ref[k:k + 1, :] * shifted[b][a8:a8 + GRID_W]
        return acc

    pad_ref[0:CONV_PAD] = zeros
    pad_ref[CONV_PAD:CONV_PAD + ctx] = u_ref[0, 0:ctx]
    pad_ref[CONV_PAD + ctx:2 * CONV_PAD + ctx] = zeros
    for c0 in range(0, ctx, GRID_W):
        v_ref[0, c0:c0 + GRID_W] = line_taps(pad_ref[c0:c0 + GRID_W + 2 * CONV_PAD])

    @pl.when(j < width_tiles)
    def _along_width():
        stride = GRID_W + CONV_PAD
        pad_ref[0:CONV_PAD] = zeros
        for r in range(rows):
            base = CONV_PAD + r * stride
            pad_ref[base:base + GRID_W] = u_ref[0, ctx + r * GRID_W:ctx + (r + 1) * GRID_W]
            pad_ref[base + GRID_W:base + stride] = zeros

        def body(r, carry):
            base = pl.multiple_of(r * stride, 8)
            win = pad_ref[pl.ds(base, GRID_W + 2 * CONV_PAD)]
            out0 = pl.multiple_of(ctx + r * GRID_W, 8)
            v_ref[0, pl.ds(out0, GRID_W)] = line_taps(win)
            return carry

        lax.fori_loop(0, rows, body, 0)

    @pl.when(j >= width_tiles)
    def _along_height():
        npad = half * GRID_W
        zrow = jnp.zeros((GRID_W, cw), F32)
        for r in range(half):
            pad_ref[r * GRID_W:(r + 1) * GRID_W] = zrow
            lo = npad + (rows + r) * GRID_W
            pad_ref[lo:lo + GRID_W] = zrow
        pad_ref[npad:npad + rows * GRID_W] = u_ref[0, ctx:ctx + rows * GRID_W]

        def body(r, carry):
            acc = bias
            for k in range(kw):
                src = pl.multiple_of((r + k) * GRID_W, 8)
                acc = acc + w_ref[k:k + 1, :] * pad_ref[pl.ds(src, GRID_W)]
            out0 = pl.multiple_of(ctx + r * GRID_W, 8)
            v_ref[0, pl.ds(out0, GRID_W)] = acc
            return carry

        lax.fori_loop(0, rows, body, 0)


def _dwconv(u, w, b, ctx):
    nb, l, d = u.shape
    kw = w.shape[0]
    cw = 256
    rows = (l - ctx) // GRID_W
    half = (kw - 1) // 2
    pad_rows = max(2 * CONV_PAD + ctx, CONV_PAD + rows * (GRID_W + CONV_PAD),
                   (rows + 2 * half) * GRID_W)
    kern = functools.partial(_conv_kernel, ctx=ctx, rows=rows, width_tiles=(d // 2) // cw)
    return pl.pallas_call(
        kern,
        out_shape=jax.ShapeDtypeStruct((nb, l, d), F32),
        grid=(nb, d // cw),
        in_specs=[
            pl.BlockSpec((1, l, cw), lambda b_, j: (b_, 0, j)),
            pl.BlockSpec((kw, cw), lambda b_, j: (0, j)),
            pl.BlockSpec((1, cw), lambda b_, j: (0, j)),
        ],
        out_specs=pl.BlockSpec((1, l, cw), lambda b_, j: (b_, 0, j)),
        scratch_shapes=[pltpu.VMEM((pad_rows, cw), F32)],
        compiler_params=_cp("parallel", "arbitrary"),
        name="axial_dwconv",
    )(u, w, b.reshape(1, d))


def _pw2_kernel(v_ref, h_ref, mod_ref, lng_ref, lnb_ref, w_ref, b_ref, o_ref):
    v = v_ref[0]
    mu = jnp.mean(v, axis=-1, keepdims=True)
    vc = v - mu
    var = jnp.mean(vc * vc, axis=-1, keepdims=True)
    y = _silu(vc * lax.rsqrt(var + EPS) * lng_ref[...] + lnb_ref[...]).astype(BF16)
    out = _dot(y, w_ref[...]) + b_ref[...]
    o_ref[0] = h_ref[0] + mod_ref[0][2:3] * out


def _pw2_residual(v, h, mods, ln_g, ln_b, w, j, b, ctx_tiles, t0):
    nb, l, d = h.shape
    tok, mod = _row_specs(nb, d, t0, ctx_tiles)
    return pl.pallas_call(
        _pw2_kernel,
        out_shape=jax.ShapeDtypeStruct((nb, l, d), F32),
        grid=(nb, l // TM - t0),
        in_specs=[tok(d), tok(d), mod, _full((1, d)), _full((1, d)), _layer((d, d), j), _full((1, d))],
        out_specs=tok(d),
        input_output_aliases={1: 0},
        compiler_params=_cp("parallel", "arbitrary"),
        name="conv_pw2_residual",
    )(v, h, mods, ln_g.reshape(1, d), ln_b.reshape(1, d), w, b.reshape(1, d))


def _swiglu_hidden(a, w1_ref, w3_ref, hm_ref):
    f = hm_ref.shape[-1]
    cw = 256
    for c in range(0, f, cw):
        h1 = _dot(a, w1_ref[:, c:c + cw])
        h3 = _dot(a, w3_ref[:, c:c + cw])
        hm_ref[:, c:c + cw] = (_silu(h1) * h3).astype(BF16)


def _ffn_kernel(h_ref, mod_ref, g_ref, w1_ref, w3_ref, w2_ref, o_ref, hm_ref):
    m = mod_ref[0]
    h = h_ref[0]
    a = _rms_mod(h, g_ref[...], m[3:4], m[4:5]).astype(BF16)
    _swiglu_hidden(a, w1_ref, w3_ref, hm_ref)
    o_ref[0] = h + m[5:6] * _dot(hm_ref[...], w2_ref[...])


def _ffn_dense(h, mods, g, w1, w3, w2, j, ctx_tiles):
    nb, l, d = h.shape
    f = w1.shape[-1]
    tok, mod = _row_specs(nb, d, 0, ctx_tiles)
    return pl.pallas_call(
        _ffn_kernel,
        out_shape=jax.ShapeDtypeStruct((nb, l, d), F32),
        grid=(nb, l // TM),
        in_specs=[tok(d), mod, _full((1, d)), _layer((d, f), j), _layer((d, f), j), _layer((f, d), j)],
        out_specs=tok(d),
        scratch_shapes=[pltpu.VMEM((TM, f), BF16)],
        input_output_aliases={0: 0},
        compiler_params=_cp("parallel", "arbitrary"),
        name="ffn_swiglu",
    )(h, mods, g.reshape(1, d), w1, w3, w2)


def _win_kernel(x_ref, mod_ref, g_ref, w_ref, lb_ref, p_ref):
    m = mod_ref[0]
    a = _rms_mod(x_ref[0], g_ref[...], m[0:1], m[1:2]).astype(BF16)
    d = x_ref.shape[-1]
    cw = 512
    for c in range(0, 5 * d, cw):
        part, off = divmod(c, d)
        y = _dot(a, w_ref[:, c:c + cw])
        if part in (1, 2):
            lb = lb_ref[part - 1:part, off:off + cw]
            y = jnp.log(lb + (1.0 - lb) * _sigmoid(y))
        elif part == 3:
            y = _silu(y) * (HEAD_DIM ** -0.5)
        elif part == 4:
            y = _silu(y)
        p_ref[0, :, c:c + cw] = y


def _rec_in_proj(h, mods, g, w, j, lower, ctx_tiles):
    nb, l, d = h.shape
    n = w.shape[-1]
    tok, mod = _row_specs(nb, d, 0, ctx_tiles)
    return pl.pallas_call(
        _win_kernel,
        out_shape=jax.ShapeDtypeStruct((nb, l, n), F32),
        grid=(nb, l // TM),
        in_specs=[tok(d), mod, _full((1, d)), _layer((d, n), j), _full((2, d))],
        out_specs=tok(n),
        compiler_params=_cp("parallel", "arbitrary"),
        name="rec_in_proj",
    )(h, mods, g.reshape(1, d), w, lower)


N_CHUNKS = BLK // CHUNK
N_CHUNK_SUMS = 7
SUM_ROWS = BLK + N_CHUNK_SUMS * N_CHUNKS * 8 + 8


def _rec_constants():
    t = np.arange(BLK)
    mats, masks = [], []
    for direction in range(2):
        p = t if direction == 0 else BLK - 1 - t
        pt, ps = p[:, None], p[None, :]
        ct, cs = pt // CHUNK, ps // CHUNK
        same = ct == cs
        mid4 = N_CHUNKS // 2
        mid2 = (ct // 4) * 4 + 2
        chunk_level = [
            same,
            cs < ct,
            cs > ct,
            (ct >= mid4) & (cs >= mid4) & (cs < ct),
            (ct < mid4) & (cs > ct) & (cs < mid4),
            (ct >= mid2) & (cs >= mid2) & (cs < ct),
            (ct < mid2) & (cs > ct) & (cs < mid2),
        ]
        rows = [same & (ps <= pt)]
        for m in chunk_level:
            m = np.broadcast_to(m, (BLK, BLK))
            rows.append(np.repeat(m[::CHUNK], 8, axis=0))
        rows.append(np.ones((8, BLK), bool))
        mats.append(np.concatenate([r.astype(np.float32) for r in rows], axis=0))
        m = [
            same & (ps <= pt),
            (ct % 2 == 1) & (cs == ct - 1),
            (ct // 4 == cs // 4) & (ct % 4 >= 2) & (cs % 4 < 2),
            (ct >= mid4) & (cs < mid4),
        ]
        level = np.full((BLK, BLK), float(len(m)), np.float32)
        for i, x in enumerate(m):
            level = np.where(np.broadcast_to(x, (BLK, BLK)), np.float32(i), level)
        masks.append(level)
    return np.stack(mats), np.stack(masks)


def _chunk_scale(x, ec, i):
    base = i * N_CHUNKS * 8
    parts = []
    for j in range(N_CHUNKS):
        e = ec[base + 8 * j:base + 8 * j + 8]
        parts.append(x[CHUNK * j:CHUNK * j + 8] * e)
        parts.append(x[CHUNK * j + 8:CHUNK * j + 16] * e)
    return jnp.concatenate(parts, axis=0)


def _gla_kernel(v_ref, lff_ref, lfb_ref, q_ref, a_ref, m_ref, of_ref, ob_ref, *, ctx_blocks, n_blocks):
    last = n_blocks - 1
    n_heads = v_ref.shape[-1] // HEAD_DIM
    chains = [(dr, hd) for dr in range(2) for hd in range(n_heads)]
    lf_refs = (lff_ref, lfb_ref)
    o_refs = (of_ref, ob_ref)

    def step(n, states):
        rev = jnp.where(n < ctx_blocks, ctx_blocks - 1 - n, last + ctx_blocks - n)
        row0 = (pl.multiple_of(n * BLK, BLK), pl.multiple_of(rev * BLK, BLK))

        sums = []
        for dr in range(2):
            logf = lf_refs[dr][0, pl.ds(row0[dr], BLK), :]
            hi = logf.astype(BF16)
            lo = (logf - hi.astype(F32)).astype(BF16)
            sums.append((logf, _dot(a_ref[dr], jnp.concatenate([hi, lo], axis=0))))

        ops = []
        for dr, hd in chains:
            cols = slice(hd * HEAD_DIM, (hd + 1) * HEAD_DIM)
            logf, s = sums[dr]
            s = s[:, cols]
            k = 1.0 - jnp.exp(logf[:, cols])
            qs = q_ref[0, pl.ds(row0[dr], BLK), cols]
            bcum = s[0:BLK]
            ec = jnp.exp(s[BLK:SUM_ROWS])
            q_in = qs * jnp.exp(bcum)
            k_in = k * jnp.exp(-bcum)
            k_st = _chunk_scale(k_in, ec, 0)
            ops.append(dict(
                q_blk=_chunk_scale(q_in, ec, 1).astype(BF16),
                k_end=_chunk_scale(k_st, ec, 2).astype(BF16),
                q4=_chunk_scale(q_in, ec, 3).astype(BF16),
                k4=_chunk_scale(k_st, ec, 4).astype(BF16),
                q2=_chunk_scale(q_in, ec, 5).astype(BF16),
                k2=_chunk_scale(k_st, ec, 6).astype(BF16),
                e_tot=ec[N_CHUNK_SUMS * N_CHUNKS * 8:N_CHUNK_SUMS * N_CHUNKS * 8 + 1],
                q_in=q_in.astype(BF16), k_in=k_in.astype(BF16), k_st=k_st.astype(BF16),
                vb=v_ref[0, pl.ds(row0[dr], BLK), cols].astype(BF16)))

        raw = [(_dot_nt(o["q_in"], o["k_in"]), _dot_nt(o["q_in"], o["k_st"]),
                _dot_nt(o["q2"], o["k2"]), _dot_nt(o["q4"], o["k4"]),
                _dot_nt(o["q_blk"], st.astype(BF16)), _dot_tn(o["vb"], o["k_end"]))
               for o, st in zip(ops, states)]

        atts = []
        for (dr, hd), r in zip(chains, raw):
            level = m_ref[dr]
            att = jnp.where(level == 0.0, r[0], jnp.where(level == 1.0, r[1], jnp.where(
                level == 2.0, r[2], jnp.where(level == 3.0, r[3], 0.0))))
            atts.append(att.astype(BF16))

        new_states = []
        for (dr, hd), o, r, att, st in zip(chains, ops, raw, atts, states):
            cols = slice(hd * HEAD_DIM, (hd + 1) * HEAD_DIM)
            o_refs[dr][0, pl.ds(row0[dr], BLK), cols] = _dot(att, o["vb"]) + r[4]
            new_states.append(o["e_tot"] * st + r[5])
        return tuple(new_states)

    zero = jnp.zeros((HEAD_DIM, HEAD_DIM), F32)
    lax.fori_loop(0, n_blocks, step, (zero,) * len(chains))


GLA_HEADS_PER_STEP = 2


def _gla(p, ctx):
    nb, l, n5 = p.shape
    d = n5 // 5
    cw = GLA_HEADS_PER_STEP * HEAD_DIM
    groups = d // cw
    amat, masks = _rec_constants()
    amat = np.concatenate([amat, amat], axis=2)
    kern = functools.partial(_gla_kernel, ctx_blocks=ctx // BLK, n_blocks=l // BLK)
    col = lambda base: pl.BlockSpec((1, l, cw), lambda b, h: (b, 0, base + h))
    out = jax.ShapeDtypeStruct((nb, l, d), F32)
    return pl.pallas_call(
        kern,
        out_shape=(out, out),
        grid=(nb, groups),
        in_specs=[col(0), col(groups), col(2 * groups), col(3 * groups),
                  _full(amat.shape), _full(masks.shape)],
        out_specs=(col(0), col(0)),
        compiler_params=_cp("parallel", "arbitrary"),
        name="hgrn2_gla",
    )(p, p, p, p, jnp.asarray(amat, BF16), jnp.asarray(masks, F32))


def _readout_kernel(of_ref, ob_ref, g_ref, h_ref, mod_ref, ng_ref, w_ref, out_ref, y_ref):
    d = of_ref.shape[-1]
    for c in range(0, d, HEAD_DIM):
        o = of_ref[0, :, c:c + HEAD_DIM] + ob_ref[0, :, c:c + HEAD_DIM]
        o = o * lax.rsqrt(jnp.mean(o * o, axis=-1, keepdims=True) + EPS) * ng_ref[:, c:c + HEAD_DIM]
        y_ref[:, c:c + HEAD_DIM] = (o * g_ref[0, :, c:c + HEAD_DIM]).astype(BF16)
    out_ref[0] = h_ref[0] + mod_ref[0][2:3] * _dot(y_ref[...], w_ref[...])


def _rec_readout(o_f, o_b, p, h, mods, onorm_g, w_o, j, ctx_tiles, t0):
    nb, l, d = h.shape
    tok, mod = _row_specs(nb, d, t0, ctx_tiles)
    gate_cols = pl.BlockSpec((1, TM, d), lambda b, t: (b, t + t0, 4))
    return pl.pallas_call(
        _readout_kernel,
        out_shape=jax.ShapeDtypeStruct((nb, l, d), F32),
        grid=(nb, l // TM - t0),
        in_specs=[tok(d), tok(d), gate_cols, tok(d), mod, _full((1, d)), _layer((d, d), j)],
        out_specs=tok(d),
        scratch_shapes=[pltpu.VMEM((TM, d), BF16)],
        input_output_aliases={3: 0},
        compiler_params=_cp("parallel", "arbitrary"),
        name="rec_readout",
    )(o_f, o_b, p, h, mods, onorm_g.reshape(1, d), w_o)


def _route_kernel(h_ref, mod_ref, g_ref, r_ref, tri_ref, a_ref, ri_ref, rw_ref, cnt_ref, carry_ref,
                  *, n_experts):
    @pl.when((pl.program_id(0) == 0) & (pl.program_id(1) == 0))
    def _():
        carry_ref[...] = jnp.zeros_like(carry_ref)

    m = mod_ref[0]
    a = _rms_mod(h_ref[0], g_ref[...], m[3:4], m[4:5])
    a_ref[0] = a
    logits = jnp.dot(a, r_ref[...], preferred_element_type=F32, precision=lax.Precision.HIGHEST)
    lane = lax.broadcasted_iota(jnp.int32, logits.shape, 1)
    neg = jnp.float32(-jnp.inf)
    logits = jnp.where(lane < n_experts, logits, neg)
    v1 = jnp.max(logits, axis=-1, keepdims=True)
    i1 = jnp.min(jnp.where(logits == v1, lane, LANES), axis=-1, keepdims=True)
    rest = jnp.where(lane == i1, neg, logits)
    v2 = jnp.max(rest, axis=-1, keepdims=True)
    i2 = jnp.min(jnp.where(rest == v2, lane, LANES), axis=-1, keepdims=True)
    e2 = jnp.exp(v2 - v1)
    w1 = 1.0 / (1.0 + e2)
    w2 = e2 / (1.0 + e2)
    oh1 = jnp.where(lane == i1, 1.0, 0.0)
    oh2 = jnp.where(lane == i2, 1.0, 0.0)
    both = oh1 + oh2
    earlier = _dot(tri_ref[...], both.astype(BF16)) + carry_ref[...]
    r1 = jnp.sum(oh1 * earlier, axis=-1, keepdims=True).astype(jnp.int32)
    r2 = jnp.sum(oh2 * earlier, axis=-1, keepdims=True).astype(jnp.int32)
    carry = carry_ref[...] + jnp.sum(both, axis=0, keepdims=True)
    carry_ref[...] = carry
    cnt_ref[...] = jnp.broadcast_to(carry, cnt_ref.shape).astype(jnp.int32)
    ri_ref[0] = jnp.where(lane == 0, i1, jnp.where(lane == 1, i2, jnp.where(lane == 2, r1,
                          jnp.where(lane == 3, r2, 0))))
    rw_ref[0] = jnp.where(lane == 0, w1, jnp.where(lane == 1, w2, 0.0))


def _moe_route(h, mods, g, router, ctx_tiles, t0):
    nb, l, d = h.shape
    n_experts = router.shape[-1]
    nt = l // TM - t0
    tok, mod = _row_specs(nb, d, t0, ctx_tiles)
    out_tok = lambda width: pl.BlockSpec((1, TM, width), lambda b, t: (b, t, 0))
    rpad = jnp.pad(router, ((0, 0), (0, LANES - n_experts)))
    tri = jnp.asarray(np.tril(np.ones((TM, TM), np.float32), -1), BF16)
    return pl.pallas_call(
        functools.partial(_route_kernel, n_experts=n_experts),
        out_shape=(jax.ShapeDtypeStruct((nb, nt * TM, d), F32),
                   jax.ShapeDtypeStruct((nb, nt * TM, LANES), jnp.int32),
                   jax.ShapeDtypeStruct((nb, nt * TM, LANES), F32),
                   jax.ShapeDtypeStruct((8, LANES), jnp.int32)),
        grid=(nb, nt),
        in_specs=[tok(d), mod, _full((1, d)), _full((d, LANES)), _full((TM, TM))],
        out_specs=(out_tok(d), out_tok(LANES), out_tok(LANES), _full((8, LANES))),
        scratch_shapes=[pltpu.VMEM((1, LANES), F32)],
        compiler_params=_cp("arbitrary", "arbitrary"),
        name="moe_route",
    )(h, mods, g.reshape(1, d), rpad, tri)


def _row_copy(src, i, dst, j, sem):
    return pltpu.make_async_copy(src.at[pl.ds(i, 1)], dst.at[pl.ds(j, 1)], sem)


def _dispatch_kernel(pos_ref, pad_ref, nv_ref, a_ref, xs_hbm, zero_ref, sems, *, n_tok_tiles):
    step = pl.program_id(0)
    base = step * TM
    tile = a_ref.at[0]

    for r in range(TM):
        t = base + r
        _row_copy(tile, r, xs_hbm, pos_ref[2 * t], sems.at[0]).start()
        _row_copy(tile, r, xs_hbm, pos_ref[2 * t + 1], sems.at[0]).start()
    for _ in range(TOP_K):
        pltpu.make_async_copy(tile, xs_hbm.at[pl.ds(0, TM)], sems.at[0]).wait()

    @pl.when(step == n_tok_tiles - 1)
    def _fill():
        zero_ref[...] = jnp.zeros_like(zero_ref)

        def pad_start(i, carry):
            @pl.when(pad_ref[i] >= 0)
            def _():
                _row_copy(zero_ref, 0, xs_hbm, pad_ref[i], sems.at[1]).start()
            return carry

        def pad_wait(i, carry):
            @pl.when(pad_ref[i] >= 0)
            def _():
                _row_copy(zero_ref, 0, xs_hbm, 0, sems.at[1]).wait()
            return carry

        lax.fori_loop(0, pad_ref.shape[0], pad_start, 0)
        lax.fori_loop(0, pad_ref.shape[0], pad_wait, 0)

        n_tiles = xs_hbm.shape[0] // TM
        n_tail = n_tiles - TOP_K * n_tok_tiles

        def tail_copy(i):
            row0 = pl.multiple_of((nv_ref[0] + i) * TM, TM)
            return pltpu.make_async_copy(zero_ref, xs_hbm.at[pl.ds(row0, TM)], sems.at[1])

        def tail_start(i, carry):
            @pl.when(nv_ref[0] + i < n_tiles)
            def _():
                tail_copy(i).start()
            return carry

        def tail_wait(i, carry):
            @pl.when(nv_ref[0] + i < n_tiles)
            def _():
                tail_copy(i).wait()
            return carry

        lax.fori_loop(0, n_tail, tail_start, 0)
        lax.fori_loop(0, n_tail, tail_wait, 0)


def _moe_dispatch(a, pos, pad_rows, n_valid, n_rows):
    nb, n, d = a.shape
    nt = n // TM
    kern = functools.partial(_dispatch_kernel, n_tok_tiles=nb * nt)
    return pl.pallas_call(
        kern,
        out_shape=jax.ShapeDtypeStruct((n_rows, d), F32),
        grid_spec=pltpu.PrefetchScalarGridSpec(
            num_scalar_prefetch=3, grid=(nb * nt,),
            in_specs=[pl.BlockSpec((1, TM, d), lambda i, p, q, v: (i // nt, i % nt, 0))],
            out_specs=pl.BlockSpec(memory_space=pl.ANY),
            scratch_shapes=[pltpu.VMEM((TM, d), F32), pltpu.SemaphoreType.DMA((2,))]),
        compiler_params=_cp("arbitrary"),
        name="moe_dispatch",
    )(pos, pad_rows, n_valid, a)


def _experts_kernel(te_ref, xi_ref, nv_ref, x_ref, w1_ref, w3_ref, w2_ref, y_ref, hm_ref):
    @pl.when(pl.program_id(0) < nv_ref[0])
    def _():
        _swiglu_hidden(x_ref[...].astype(BF16), w1_ref.at[0], w3_ref.at[0], hm_ref)
        y_ref[...] = _dot(hm_ref[...], w2_ref[0])

    @pl.when(pl.program_id(0) >= nv_ref[0])
    def _():
        y_ref[...] = jnp.zeros_like(y_ref)


def _moe_experts(xs, tile_expert, x_index, n_valid, w1, w3, w2, j, n_tiles):
    d = xs.shape[-1]
    f = w1.shape[-1]
    rows = pl.BlockSpec((TM, d), lambda i, te, xi, nv: (xi[i], 0))
    wspec = lambda s: pl.BlockSpec((None, 1) + s, lambda i, te, xi, nv: (j, te[i], 0, 0))
    return pl.pallas_call(
        _experts_kernel,
        out_shape=jax.ShapeDtypeStruct((n_tiles * TM, d), F32),
        grid_spec=pltpu.PrefetchScalarGridSpec(
            num_scalar_prefetch=3, grid=(n_tiles,),
            in_specs=[rows, wspec((d, f)), wspec((d, f)), wspec((f, d))],
            out_specs=pl.BlockSpec((TM, d), lambda i, te, xi, nv: (i, 0)),
            scratch_shapes=[pltpu.VMEM((TM, f), BF16)]),
        compiler_params=_cp("arbitrary"),
        name="moe_experts",
    )(tile_expert, x_index, n_valid, xs, w1, w3, w2)


def _combine_kernel(pos_ref, h_ref, mod_ref, rw_ref, ys_hbm, o_ref, y1_ref, y2_ref, sems, *, nt, fg_ref=None):
    base = (pl.program_id(0) * nt + pl.program_id(1)) * TM

    for r in range(TM):
        t = base + r
        _row_copy(ys_hbm, pos_ref[2 * t], y1_ref, r, sems.at[0]).start()
        _row_copy(ys_hbm, pos_ref[2 * t + 1], y2_ref, r, sems.at[1]).start()
    pltpu.make_async_copy(ys_hbm.at[pl.ds(0, TM)], y1_ref, sems.at[0]).wait()
    pltpu.make_async_copy(ys_hbm.at[pl.ds(0, TM)], y2_ref, sems.at[1]).wait()
    rw = rw_ref[0]
    y = rw[:, 0:1] * y1_ref[...] + rw[:, 1:2] * y2_ref[...]
    out = h_ref[0] + mod_ref[0][5:6] * y
    if fg_ref is not None:
        out = out * lax.rsqrt(jnp.mean(out * out, axis=-1, keepdims=True) + EPS) * fg_ref[...]
    o_ref[0] = out


def _combine_final_kernel(pos_ref, h_ref, mod_ref, rw_ref, fg_ref, ys_hbm, o_ref, y1_ref, y2_ref, sems, *, nt):
    _combine_kernel(pos_ref, h_ref, mod_ref, rw_ref, ys_hbm, o_ref, y1_ref, y2_ref, sems, nt=nt, fg_ref=fg_ref)


def _moe_combine(ys, pos, rw, h, mods, ctx_tiles, t0, final_g=None):
    nb, l, d = h.shape
    nt = l // TM - t0
    tok = lambda width, off: pl.BlockSpec((1, TM, width), lambda b, t, p: (b, t + off, 0))
    mod = pl.BlockSpec((1, 6, d), lambda b, t, p: (jnp.where(t + t0 < ctx_tiles, nb, b), 0, 0))
    scratch = [pltpu.VMEM((TM, d), F32), pltpu.VMEM((TM, d), F32), pltpu.SemaphoreType.DMA((2,))]
    any_spec = pl.BlockSpec(memory_space=pl.ANY)
    if final_g is None:
        return pl.pallas_call(
            functools.partial(_combine_kernel, nt=nt),
            out_shape=jax.ShapeDtypeStruct((nb, l, d), F32),
            grid_spec=pltpu.PrefetchScalarGridSpec(
                num_scalar_prefetch=1, grid=(nb, nt),
                in_specs=[tok(d, t0), mod, tok(LANES, 0), any_spec],
                out_specs=tok(d, t0), scratch_shapes=scratch),
            input_output_aliases={1: 0},
            compiler_params=_cp("arbitrary", "arbitrary"),
            name="moe_combine",
        )(pos, h, mods, rw, ys)
    return pl.pallas_call(
        functools.partial(_combine_final_kernel, nt=nt),
        out_shape=jax.ShapeDtypeStruct((nb, nt * TM, d), F32),
        grid_spec=pltpu.PrefetchScalarGridSpec(
            num_scalar_prefetch=1, grid=(nb, nt),
            in_specs=[tok(d, t0), mod, tok(LANES, 0), pl.BlockSpec((1, d), lambda b, t, p: (0, 0)), any_spec],
            out_specs=tok(d, 0), scratch_shapes=scratch),
        compiler_params=_cp("arbitrary", "arbitrary"),
        name="moe_combine_final",
    )(pos, h, mods, rw, final_g.reshape(1, d), ys)


def _moe(h, mods, g, router, w1, w3, w2, j, ctx_tiles, t0, final_g=None):
    n_experts = router.shape[-1]
    a, ri, rw, counts = _moe_route(h, mods, g, router, ctx_tiles, t0)
    nb, n, d = a.shape
    tokens = nb * n
    n_tiles = (TOP_K * tokens) // TM + n_experts
    counts = counts[0, :n_experts]
    padded = ((counts + TM - 1) // TM) * TM
    ends = jnp.cumsum(padded)
    starts = ends - padded
    ri = ri.reshape(tokens, LANES)
    pos = (starts[ri[:, 0:TOP_K]] + ri[:, TOP_K:2 * TOP_K]).reshape(TOP_K * tokens)
    n_valid = ends[-1] // TM
    x_index = jnp.maximum(jnp.minimum(jnp.arange(n_tiles, dtype=jnp.int32), n_valid - 1), 0)
    tile_expert = jnp.minimum(jnp.sum(ends[None, :] <= (x_index * TM)[:, None], axis=1), n_experts - 1)
    fill = jnp.arange(TM, dtype=jnp.int32)[None, :]
    pad_rows = jnp.where(fill < (padded - counts)[:, None], (starts + counts)[:, None] + fill, -1)
    n_valid = n_valid.reshape(1).astype(jnp.int32)
    xs = _moe_dispatch(a, pos.astype(jnp.int32),
                       pad_rows.reshape(-1).astype(jnp.int32), n_valid, n_tiles * TM)
    ys = _moe_experts(xs, tile_expert.astype(jnp.int32), x_index, n_valid, w1, w3, w2, j, n_tiles)
    return _moe_combine(ys, pos.astype(jnp.int32), rw, h, mods, ctx_tiles, t0, final_g)


def kernel(x, c, ctx, c_ctx, ada_w, ada_b, norm_mix_g, norm_ffn_g, final_g, conv_pw1_w, conv_pw1_b, conv_dw_w, conv_dw_b, conv_ln_g, conv_ln_b, conv_pw2_w, conv_pw2_b, rec_w_in, rec_lb_logits, rec_onorm_g, rec_w_o, ffn_w1, ffn_w3, ffn_w2, moe_router, moe_w1, moe_w3, moe_w2):
    nb, seq, d = x.shape
    n_ctx = ctx.shape[1]
    depth = ada_w.shape[0]
    ctx_tiles = n_ctx // TM
    assert n_ctx % TM == 0 and seq % TM == 0 and n_ctx % BLK == 0 and seq % (GRID_W * 8) == 0
    assert d % (2 * 256) == 0 and nb < 16 and depth % 2 == 0

    cond = jnp.zeros((16, d), F32).at[:nb].set(c).at[nb].set(c_ctx)
    mods = _modulation(cond, ada_w, ada_b)
    sm = jax.nn.softmax(rec_lb_logits.astype(F32), axis=1)
    lower = jnp.cumsum(sm, axis=1) - sm[:, :1]

    bf = lambda w: w.astype(BF16)
    conv_pw1_w, conv_pw2_w, rec_w_in, rec_w_o = bf(conv_pw1_w), bf(conv_pw2_w), bf(rec_w_in), bf(rec_w_o)
    ffn_w1, ffn_w3, ffn_w2 = bf(ffn_w1), bf(ffn_w3), bf(ffn_w2)
    moe_w1, moe_w3, moe_w2 = bf(moe_w1), bf(moe_w3), bf(moe_w2)

    h = jnp.concatenate([ctx, x], axis=1)
    for i in range(depth):
        last = i == depth - 1
        j = i // 2
        t0 = ctx_tiles if last else 0
        if i % 2 == 0:
            u = _pw1_glu(h, mods[i], norm_mix_g[i], conv_pw1_w, j, conv_pw1_b[j], ctx_tiles)
            v = _dwconv(u, conv_dw_w[j], conv_dw_b[j], n_ctx)
            h = _pw2_residual(v, h, mods[i], conv_ln_g[j], conv_ln_b[j], conv_pw2_w, j, conv_pw2_b[j],
                              ctx_tiles, t0)
            h = _ffn_dense(h, mods[i], norm_ffn_g[i], ffn_w1, ffn_w3, ffn_w2, j, ctx_tiles)
        else:
            p = _rec_in_proj(h, mods[i], norm_mix_g[i], rec_w_in, j, lower[:, i], ctx_tiles)
            o_f, o_b = _gla(p, n_ctx)
            h = _rec_readout(o_f, o_b, p, h, mods[i], rec_onorm_g[j], rec_w_o, j, ctx_tiles, t0)
            h = _moe(h, mods[i], norm_ffn_g[i], moe_router[j], moe_w1, moe_w3, moe_w2, j, ctx_tiles, t0,
                     final_g if last else None)
    return h
```

```python
import functools

import numpy as np
import jax
import jax.numpy as jnp
from jax import lax
from jax.experimental import pallas as pl
from jax.experimental.pallas import tpu as pltpu

EPS = 1e-6
GRID_W = 64
CONV_PAD = 16
HEAD_DIM = 128
BLK = 128
CHUNK = 16
TOP_K = 2
TM = 256
LANES = 128
VMEM_LIMIT = 56 * 1024 * 1024

F32 = jnp.float32
BF16 = jnp.bfloat16


def _cp(*sem):
    return pltpu.CompilerParams(dimension_semantics=sem, vmem_limit_bytes=VMEM_LIMIT)


def _dot(a, b):
    return jnp.dot(a, b, preferred_element_type=F32)


def _dot_nt(a, b):
    return lax.dot_general(a, b, (((1,), (1,)), ((), ())), preferred_element_type=F32)


def _dot_tn(a, b):
    return lax.dot_general(a, b, (((0,), (0,)), ((), ())), preferred_element_type=F32)


def _sigmoid(x):
    return 1.0 / (1.0 + jnp.exp(-x))


def _silu(x):
    return x * _sigmoid(x)


def _rms_mod(x, g, shift, scale):
    ms = jnp.mean(x * x, axis=-1, keepdims=True)
    return (x * lax.rsqrt(ms + EPS) * g) * (1.0 + scale) + shift


def _mod_kernel(cond_ref, w_ref, b_ref, o_ref):
    s = _silu(cond_ref[...])
    o_ref[0] = jnp.dot(s, w_ref[0], preferred_element_type=F32,
                       precision=lax.Precision.HIGHEST) + b_ref[0]


def _modulation(cond, ada_w, ada_b):
    depth, d, n = ada_w.shape
    tn = n // 4
    out = pl.pallas_call(
        _mod_kernel,
        out_shape=jax.ShapeDtypeStruct((depth, cond.shape[0], n), F32),
        grid=(depth, n // tn),
        in_specs=[
            pl.BlockSpec(cond.shape, lambda l, j: (0, 0)),
            pl.BlockSpec((1, d, tn), lambda l, j: (l, 0, j)),
            pl.BlockSpec((1, 1, tn), lambda l, j: (l, 0, j)),
        ],
        out_specs=pl.BlockSpec((1, cond.shape[0], tn), lambda l, j: (l, 0, j)),
        compiler_params=_cp("arbitrary", "arbitrary"),
        name="modulation",
    )(cond, ada_w, ada_b.reshape(depth, 1, n))
    return out.reshape(depth, cond.shape[0], 6, d)


def _row_specs(nb, d, t0, ctx_tiles):
    tok = lambda width: pl.BlockSpec((1, TM, width), lambda b, t: (b, t + t0, 0))
    mod = pl.BlockSpec((1, 6, d), lambda b, t: (jnp.where(t + t0 < ctx_tiles, nb, b), 0, 0))
    return tok, mod


def _full(shape):
    return pl.BlockSpec(shape, lambda b, t: (0,) * len(shape))


def _layer(shape, j):
    return pl.BlockSpec((None,) + shape, lambda b, t: (j,) + (0,) * len(shape))


def _pw1_kernel(x_ref, mod_ref, g_ref, w_ref, b_ref, u_ref):
    m = mod_ref[0]
    a = _rms_mod(x_ref[0], g_ref[...], m[0:1], m[1:2]).astype(BF16)
    d = u_ref.shape[-1]
    cw = 512
    for c in range(0, d, cw):
        a1 = _dot(a, w_ref[:, c:c + cw]) + b_ref[:, c:c + cw]
        a2 = _dot(a, w_ref[:, d + c:d + c + cw]) + b_ref[:, d + c:d + c + cw]
        u_ref[0, :, c:c + cw] = a1 * _sigmoid(a2)


def _pw1_glu(h, mods, g, w, j, b, ctx_tiles):
    nb, l, d = h.shape
    tok, mod = _row_specs(nb, d, 0, ctx_tiles)
    return pl.pallas_call(
        _pw1_kernel,
        out_shape=jax.ShapeDtypeStruct((nb, l, d), F32),
        grid=(nb, l // TM),
        in_specs=[tok(d), mod, _full((1, d)), _layer((d, 2 * d), j), _full((1, 2 * d))],
        out_specs=tok(d),
        compiler_params=_cp("parallel", "arbitrary"),
        name="conv_pw1_glu",
    )(h, mods, g.reshape(1, d), w, b.reshape(1, 2 * d))


def _conv_kernel(u_ref, w_ref, b_ref, v_ref, pad_ref, *, ctx, rows, width_tiles):
    kw = w_ref.shape[0]
    half = (kw - 1) // 2
    cw = u_ref.shape[-1]
    j = pl.program_id(1)
    bias = jnp.broadcast_to(b_ref[...], (GRID_W, cw))
    zeros = jnp.zeros((CONV_PAD, cw), F32)

    def line_taps(win):
        acc = bias
        n = win.shape[0]
        shifted = {0: win}
        for k in range(kw):
            s = CONV_PAD + k - half
            b, a8 = s % 8, (s // 8) * 8
            if b not in shifted:
                shifted[b] = pltpu.roll(win, n - b, axis=0)
            acc = acc + w_ref[k:k + 1, :] * shifted[b][a8:a8 + GRID_W]
        return acc

    pad_ref[0:CONV_PAD] = zeros
    pad_ref[CONV_PAD:CONV_PAD + ctx] = u_ref[0, 0:ctx]
    pad_ref[CONV_PAD + ctx:2 * CONV_PAD + ctx] = zeros
    for c0 in range(0, ctx, GRID_W):
        v_ref[0, c0:c0 + GRID_W] = line_taps(pad_ref[c0:c0 + GRID_W + 2 * CONV_PAD])

    @pl.when(j < width_tiles)
    def _along_width():
        stride = GRID_W + CONV_PAD
        pad_ref[0:CONV_PAD] = zeros
        for r in range(rows):
            base = CONV_PAD + r * stride
            pad_ref[base:base + GRID_W] = u_ref[0, ctx + r * GRID_W:ctx + (r + 1) * GRID_W]
            pad_ref[base + GRID_W:base + stride] = zeros

        def body(r, carry):
            base = pl.multiple_of(r * stride, 8)
            win = pad_ref[pl.ds(base, GRID_W + 2 * CONV_PAD)]
            out0 = pl.multiple_of(ctx + r * GRID_W, 8)
            v_ref[0, pl.ds(out0, GRID_W)] = line_taps(win)
            return carry

        lax.fori_loop(0, rows, body, 0)

    @pl.when(j >= width_tiles)
    def _along_height():
        npad = half * GRID_W
        zrow = jnp.zeros((GRID_W, cw), F32)
        for r in range(half):
            pad_ref[r * GRID_W:(r + 1) * GRID_W] = zrow
            lo = npad + (rows + r) * GRID_W
            pad_ref[lo:lo + GRID_W] = zrow
        pad_ref[npad:npad + rows * GRID_W] = u_ref[0, ctx:ctx + rows * GRID_W]

        def body(r, carry):
            acc = bias
            for k in range(kw):
                src = pl.multiple_of((r + k) * GRID_W, 8)
                acc = acc + w_ref[k:k + 1, :] * pad_ref[pl.ds(src, GRID_W)]
            out0 = pl.multiple_of(ctx + r * GRID_W, 8)
            v_ref[0, pl.ds(out0, GRID_W)] = acc
            return carry

        lax.fori_loop(0, rows, body, 0)


def _dwconv(u, w, b, ctx):
    nb, l, d = u.shape
    kw = w.shape[0]
    cw = 256
    rows = (l - ctx) // GRID_W
    half = (kw - 1) // 2
    pad_rows = max(2 * CONV_PAD + ctx, CONV_PAD + rows * (GRID_W + CONV_PAD),
                   (rows + 2 * half) * GRID_W)
    kern = functools.partial(_conv_kernel, ctx=ctx, rows=rows, width_tiles=(d // 2) // cw)
    return pl.pallas_call(
        kern,
        out_shape=jax.ShapeDtypeStruct((nb, l, d), F32),
        grid=(nb, d // cw),
        in_specs=[
            pl.BlockSpec((1, l, cw), lambda b_, j: (b_, 0, j)),
            pl.BlockSpec((kw, cw), lambda b_, j: (0, j)),
            pl.BlockSpec((1, cw), lambda b_, j: (0, j)),
        ],
        out_specs=pl.BlockSpec((1, l, cw), lambda b_, j: (b_, 0, j)),
        scratch_shapes=[pltpu.VMEM((pad_rows, cw), F32)],
        compiler_params=_cp("parallel", "arbitrary"),
        name="axial_dwconv",
    )(u, w, b.reshape(1, d))


def _pw2_kernel(v_ref, h_ref, mod_ref, lng_ref, lnb_ref, w_ref, b_ref, o_ref):
    v = v_ref[0]
    mu = jnp.mean(v, axis=-1, keepdims=True)
    vc = v - mu
    var = jnp.mean(vc * vc, axis=-1, keepdims=True)
    y = _silu(vc * lax.rsqrt(var + EPS) * lng_ref[...] + lnb_ref[...]).astype(BF16)
    out = _dot(y, w_ref[...]) + b_ref[...]
    o_ref[0] = h_ref[0] + mod_ref[0][2:3] * out


def _pw2_residual(v, h, mods, ln_g, ln_b, w, j, b, ctx_tiles, t0):
    nb, l, d = h.shape
    tok, mod = _row_specs(nb, d, t0, ctx_tiles)
    return pl.pallas_call(
        _pw2_kernel,
        out_shape=jax.ShapeDtypeStruct((nb, l, d), F32),
        grid=(nb, l // TM - t0),
        in_specs=[tok(d), tok(d), mod, _full((1, d)), _full((1, d)), _layer((d, d), j), _full((1, d))],
        out_specs=tok(d),
        input_output_aliases={1: 0},
        compiler_params=_cp("parallel", "arbitrary"),
        name="conv_pw2_residual",
    )(v, h, mods, ln_g.reshape(1, d), ln_b.reshape(1, d), w, b.reshape(1, d))


def _swiglu_hidden(a, w1_ref, w3_ref, hm_ref):
    f = hm_ref.shape[-1]
    cw = 256
    for c in range(0, f, cw):
        h1 = _dot(a, w1_ref[:, c:c + cw])
        h3 = _dot(a, w3_ref[:, c:c + cw])
        hm_ref[:, c:c + cw] = (_silu(h1) * h3).astype(BF16)


def _ffn_kernel(h_ref, mod_ref, g_ref, w1_ref, w3_ref, w2_ref, o_ref, hm_ref):
    m = mod_ref[0]
    h = h_ref[0]
    a = _rms_mod(h, g_ref[...], m[3:4], m[4:5]).astype(BF16)
    _swiglu_hidden(a, w1_ref, w3_ref, hm_ref)
    o_ref[0] = h + m[5:6] * _dot(hm_ref[...], w2_ref[...])


def _ffn_dense(h, mods, g, w1, w3, w2, j, ctx_tiles):
    nb, l, d = h.shape
    f = w1.shape[-1]
    tok, mod = _row_specs(nb, d, 0, ctx_tiles)
    return pl.pallas_call(
        _ffn_kernel,
        out_shape=jax.ShapeDtypeStruct((nb, l, d), F32),
        grid=(nb, l // TM),
        in_specs=[tok(d), mod, _full((1, d)), _layer((d, f), j), _layer((d, f), j), _layer((f, d), j)],
        out_specs=tok(d),
        scratch_shapes=[pltpu.VMEM((TM, f), BF16)],
        input_output_aliases={0: 0},
        compiler_params=_cp("parallel", "arbitrary"),
        name="ffn_swiglu",
    )(h, mods, g.reshape(1, d), w1, w3, w2)


def _win_kernel(x_ref, mod_ref, g_ref, w_ref, lb_ref, p_ref):
    m = mod_ref[0]
    a = _rms_mod(x_ref[0], g_ref[...], m[0:1], m[1:2]).astype(BF16)
    d = x_ref.shape[-1]
    cw = 512
    for c in range(0, 5 * d, cw):
        part, off = divmod(c, d)
        y = _dot(a, w_ref[:, c:c + cw])
        if part in (1, 2):
            lb = lb_ref[part - 1:part, off:off + cw]
            y = jnp.log(lb + (1.0 - lb) * _sigmoid(y))
        elif part == 3:
            y = _silu(y) * (HEAD_DIM ** -0.5)
        elif part == 4:
            y = _silu(y)
        p_ref[0, :, c:c + cw] = y


def _rec_in_proj(h, mods, g, w, j, lower, ctx_tiles):
    nb, l, d = h.shape
    n = w.shape[-1]
    tok, mod = _row_specs(nb, d, 0, ctx_tiles)
    return pl.pallas_call(
        _win_kernel,
        out_shape=jax.ShapeDtypeStruct((nb, l, n), F32),
        grid=(nb, l // TM),
        in_specs=[tok(d), mod, _full((1, d)), _layer((d, n), j), _full((2, d))],
        out_specs=tok(n),
        compiler_params=_cp("parallel", "arbitrary"),
        name="rec_in_proj",
    )(h, mods, g.reshape(1, d), w, lower)


N_CHUNKS = BLK // CHUNK
N_CHUNK_SUMS = 7
SUM_ROWS = BLK + N_CHUNK_SUMS * N_CHUNKS + 8


def _rec_constants():
    t = np.arange(BLK)
    mats, masks = [], []
    for direction in range(2):
        p = t if direction == 0 else BLK - 1 - t
        pt, ps = p[:, None], p[None, :]
        ct, cs = pt // CHUNK, ps // CHUNK
        same = ct == cs
        mid4 = N_CHUNKS // 2
        mid2 = (ct // 4) * 4 + 2
        chunk_level = [
            same,
            cs < ct,
            cs > ct,
            (ct >= mid4) & (cs >= mid4) & (cs < ct),
            (ct < mid4) & (cs > ct) & (cs < mid4),
            (ct >= mid2) & (cs >= mid2) & (cs < ct),
            (ct < mid2) & (cs > ct) & (cs < mid2),
        ]
        rows = [same & (ps <= pt)]
        for m in chunk_level:
            m = np.broadcast_to(m, (BLK, BLK))
            rows.append(m[::CHUNK])
        rows.append(np.ones((8, BLK), bool))
        mats.append(np.concatenate([r.astype(np.float32) for r in rows], axis=0))
        m = [
            same & (ps <= pt),
            (ct % 2 == 1) & (cs == ct - 1),
            (ct // 4 == cs // 4) & (ct % 4 >= 2) & (cs % 4 < 2),
            (ct >= mid4) & (cs < mid4),
        ]
        level = np.full((BLK, BLK), float(len(m)), np.float32)
        for i, x in enumerate(m):
            level = np.where(np.broadcast_to(x, (BLK, BLK)), np.float32(i), level)
        masks.append(level)
    return np.stack(mats), np.stack(masks)


def _chunk_scale(x, ec, i):
    base = i * N_CHUNKS
    parts = []
    for j in range(N_CHUNKS):
        e = jnp.broadcast_to(ec[base + j:base + j + 1], (CHUNK, x.shape[1]))
        parts.append(x[CHUNK * j:CHUNK * (j + 1)] * e)
    return jnp.concatenate(parts, axis=0)


def _gla_kernel(v_ref, lff_ref, lfb_ref, q_ref, a_ref, m_ref, of_ref, ob_ref, *, ctx_blocks, n_blocks):
    last = n_blocks - 1
    n_heads = v_ref.shape[-1] // HEAD_DIM
    chains = [(dr, hd) for dr in range(2) for hd in range(n_heads)]
    lf_refs = (lff_ref, lfb_ref)
    o_refs = (of_ref, ob_ref)
    tot_row = N_CHUNK_SUMS * N_CHUNKS

    def block_rows(n):
        rev = jnp.where(n < ctx_blocks, ctx_blocks - 1 - n, last + ctx_blocks - n)
        return pl.multiple_of(n * BLK, BLK), pl.multiple_of(rev * BLK, BLK)

    def decay_sums(row0):
        sums = []
        for dr in range(2):
            logf = lf_refs[dr][0, pl.ds(row0[dr], BLK), :]
            hi = logf.astype(BF16)
            lo = (logf - hi.astype(F32)).astype(BF16)
            sums.append((logf, _dot(a_ref[dr], jnp.concatenate([hi, lo], axis=0))))
        return sums

    def operands(row0, sums):
        ops = []
        for dr, hd in chains:
            cols = slice(hd * HEAD_DIM, (hd + 1) * HEAD_DIM)
            logf, s = sums[dr]
            s = s[:, cols]
            k = 1.0 - jnp.exp(logf[:, cols])
            qs = q_ref[0, pl.ds(row0[dr], BLK), cols]
            bcum = s[0:BLK]
            ec = jnp.exp(s[BLK:SUM_ROWS])
            q_in = qs * jnp.exp(bcum)
            k_in = k * jnp.exp(-bcum)
            k_st = _chunk_scale(k_in, ec, 0)
            ops.append(dict(
                q_blk=_chunk_scale(q_in, ec, 1).astype(BF16),
                k_end=_chunk_scale(k_st, ec, 2).astype(BF16),
                q4=_chunk_scale(q_in, ec, 3).astype(BF16),
                k4=_chunk_scale(k_st, ec, 4).astype(BF16),
                q2=_chunk_scale(q_in, ec, 5).astype(BF16),
                k2=_chunk_scale(k_st, ec, 6).astype(BF16),
                e_tot=ec[tot_row:tot_row + 1],
                q_in=q_in.astype(BF16), k_in=k_in.astype(BF16), k_st=k_st.astype(BF16),
                vb=v_ref[0, pl.ds(row0[dr], BLK), cols].astype(BF16)))
        return ops

    def products(ops, states):
        return [(_dot_nt(o["q_in"], o["k_in"]), _dot_nt(o["q_in"], o["k_st"]),
                 _dot_nt(o["q2"], o["k2"]), _dot_nt(o["q4"], o["k4"]),
                 _dot_nt(o["q_blk"], st.astype(BF16)), _dot_tn(o["vb"], o["k_end"]))
                for o, st in zip(ops, states)]

    def finish(row0, ops, raw, states):
        atts = []
        for (dr, hd), r in zip(chains, raw):
            level = m_ref[dr]
            att = jnp.where(level == 0.0, r[0], jnp.where(level == 1.0, r[1], jnp.where(
                level == 2.0, r[2], jnp.where(level == 3.0, r[3], 0.0))))
            atts.append(att.astype(BF16))
        new_states = []
        for (dr, hd), o, r, att, st in zip(chains, ops, raw, atts, states):
            cols = slice(hd * HEAD_DIM, (hd + 1) * HEAD_DIM)
            o_refs[dr][0, pl.ds(row0[dr], BLK), cols] = (_dot(att, o["vb"]) + r[4]).astype(o_refs[dr].dtype)
            new_states.append(o["e_tot"] * st + r[5])
        return tuple(new_states)

    def step(n, states):
        row0 = block_rows(n)
        ops = operands(row0, decay_sums(row0))
        return finish(row0, ops, products(ops, states), states)

    zero = jnp.zeros((HEAD_DIM, HEAD_DIM), F32)
    lax.fori_loop(0, n_blocks, step, (zero,) * len(chains))


GLA_HEADS_PER_STEP = 4


def _gla(p, ctx):
    nb, l, n5 = p.shape
    d = n5 // 5
    cw = GLA_HEADS_PER_STEP * HEAD_DIM
    groups = d // cw
    amat, masks = _rec_constants()
    amat = np.concatenate([amat, amat], axis=2)
    kern = functools.partial(_gla_kernel, ctx_blocks=ctx // BLK, n_blocks=l // BLK)
    col = lambda base: pl.BlockSpec((1, l, cw), lambda b, h: (b, 0, base + h))
    out = jax.ShapeDtypeStruct((nb, l, d), BF16)
    return pl.pallas_call(
        kern,
        out_shape=(out, out),
        grid=(nb, groups),
        in_specs=[col(0), col(groups), col(2 * groups), col(3 * groups),
                  _full(amat.shape), _full(masks.shape)],
        out_specs=(col(0), col(0)),
        compiler_params=_cp("parallel", "arbitrary"),
        name="hgrn2_gla",
    )(p, p, p, p, jnp.asarray(amat, BF16), jnp.asarray(masks, F32))


def _readout_kernel(of_ref, ob_ref, g_ref, h_ref, mod_ref, ng_ref, w_ref, out_ref, y_ref):
    d = of_ref.shape[-1]
    for c in range(0, d, HEAD_DIM):
        o = of_ref[0, :, c:c + HEAD_DIM].astype(F32) + ob_ref[0, :, c:c + HEAD_DIM].astype(F32)
        o = o * lax.rsqrt(jnp.mean(o * o, axis=-1, keepdims=True) + EPS) * ng_ref[:, c:c + HEAD_DIM]
        y_ref[:, c:c + HEAD_DIM] = (o * g_ref[0, :, c:c + HEAD_DIM]).astype(BF16)
    out_ref[0] = h_ref[0] + mod_ref[0][2:3] * _dot(y_ref[...], w_ref[...])


def _rec_readout(o_f, o_b, p, h, mods, onorm_g, w_o, j, ctx_tiles, t0):
    nb, l, d = h.shape
    tok, mod = _row_specs(nb, d, t0, ctx_tiles)
    gate_cols = pl.BlockSpec((1, TM, d), lambda b, t: (b, t + t0, 4))
    return pl.pallas_call(
        _readout_kernel,
        out_shape=jax.ShapeDtypeStruct((nb, l, d), F32),
        grid=(nb, l // TM - t0),
        in_specs=[tok(d), tok(d), gate_cols, tok(d), mod, _full((1, d)), _layer((d, d), j)],
        out_specs=tok(d),
        scratch_shapes=[pltpu.VMEM((TM, d), BF16)],
        input_output_aliases={3: 0},
        compiler_params=_cp("parallel", "arbitrary"),
        name="rec_readout",
    )(o_f, o_b, p, h, mods, onorm_g.reshape(1, d), w_o)


def _route_kernel(h_ref, mod_ref, g_ref, r_ref, tri_ref, a_ref, ri_ref, rw_ref, cnt_ref, carry_ref,
                  *, n_experts):
    @pl.when((pl.program_id(0) == 0) & (pl.program_id(1) == 0))
    def _():
        carry_ref[...] = jnp.zeros_like(carry_ref)

    m = mod_ref[0]
    a = _rms_mod(h_ref[0], g_ref[...], m[3:4], m[4:5])
    a_ref[0] = a
    logits = jnp.dot(a, r_ref[...], preferred_element_type=F32, precision=lax.Precision.HIGHEST)
    lane = lax.broadcasted_iota(jnp.int32, logits.shape, 1)
    neg = jnp.float32(-jnp.inf)
    logits = jnp.where(lane < n_experts, logits, neg)
    v1 = jnp.max(logits, axis=-1, keepdims=True)
    i1 = jnp.min(jnp.where(logits == v1, lane, LANES), axis=-1, keepdims=True)
    rest = jnp.where(lane == i1, neg, logits)
    v2 = jnp.max(rest, axis=-1, keepdims=True)
    i2 = jnp.min(jnp.where(rest == v2, lane, LANES), axis=-1, keepdims=True)
    e2 = jnp.exp(v2 - v1)
    w1 = 1.0 / (1.0 + e2)
    w2 = e2 / (1.0 + e2)
    oh1 = jnp.where(lane == i1, 1.0, 0.0)
    oh2 = jnp.where(lane == i2, 1.0, 0.0)
    both = oh1 + oh2
    earlier = _dot(tri_ref[...], both.astype(BF16)) + carry_ref[...]
    r1 = jnp.sum(oh1 * earlier, axis=-1, keepdims=True).astype(jnp.int32)
    r2 = jnp.sum(oh2 * earlier, axis=-1, keepdims=True).astype(jnp.int32)
    carry = carry_ref[...] + jnp.sum(both, axis=0, keepdims=True)
    carry_ref[...] = carry
    cnt_ref[...] = jnp.broadcast_to(carry, cnt_ref.shape).astype(jnp.int32)
    ri_ref[0] = jnp.where(lane == 0, i1, jnp.where(lane == 1, i2, jnp.where(lane == 2, r1,
                          jnp.where(lane == 3, r2, 0))))
    rw_ref[0] = jnp.where(lane == 0, w1, jnp.where(lane == 1, w2, 0.0))


def _moe_route(h, mods, g, router, ctx_tiles, t0):
    nb, l, d = h.shape
    n_experts = router.shape[-1]
    nt = l // TM - t0
    tok, mod = _row_specs(nb, d, t0, ctx_tiles)
    out_tok = lambda width: pl.BlockSpec((1, TM, width), lambda b, t: (b, t, 0))
    rpad = jnp.pad(router, ((0, 0), (0, LANES - n_experts)))
    tri = jnp.asarray(np.tril(np.ones((TM, TM), np.float32), -1), BF16)
    return pl.pallas_call(
        functools.partial(_route_kernel, n_experts=n_experts),
        out_shape=(jax.ShapeDtypeStruct((nb, nt * TM, d), F32),
                   jax.ShapeDtypeStruct((nb, nt * TM, LANES), jnp.int32),
                   jax.ShapeDtypeStruct((nb, nt * TM, LANES), F32),
                   jax.ShapeDtypeStruct((8, LANES), jnp.int32)),
        grid=(nb, nt),
        in_specs=[tok(d), mod, _full((1, d)), _full((d, LANES)), _full((TM, TM))],
        out_specs=(out_tok(d), out_tok(LANES), out_tok(LANES), _full((8, LANES))),
        scratch_shapes=[pltpu.VMEM((1, LANES), F32)],
        compiler_params=_cp("arbitrary", "arbitrary"),
        name="moe_route",
    )(h, mods, g.reshape(1, d), rpad, tri)


def _row_copy(src, i, dst, j, sem):
    return pltpu.make_async_copy(src.at[pl.ds(i, 1)], dst.at[pl.ds(j, 1)], sem)


def _dispatch_kernel(pos_ref, pad_ref, nv_ref, a_ref, xs_hbm, zero_ref, sems, *, n_tok_tiles):
    step = pl.program_id(0)
    base = step * TM
    tile = a_ref.at[0]

    for r in range(TM):
        t = base + r
        _row_copy(tile, r, xs_hbm, pos_ref[2 * t], sems.at[0]).start()
        _row_copy(tile, r, xs_hbm, pos_ref[2 * t + 1], sems.at[0]).start()
    for _ in range(TOP_K):
        pltpu.make_async_copy(tile, xs_hbm.at[pl.ds(0, TM)], sems.at[0]).wait()

    @pl.when(step == n_tok_tiles - 1)
    def _fill():
        zero_ref[...] = jnp.zeros_like(zero_ref)

        def pad_start(i, carry):
            @pl.when(pad_ref[i] >= 0)
            def _():
                _row_copy(zero_ref, 0, xs_hbm, pad_ref[i], sems.at[1]).start()
            return carry

        def pad_wait(i, carry):
            @pl.when(pad_ref[i] >= 0)
            def _():
                _row_copy(zero_ref, 0, xs_hbm, 0, sems.at[1]).wait()
            return carry

        lax.fori_loop(0, pad_ref.shape[0], pad_start, 0)
        lax.fori_loop(0, pad_ref.shape[0], pad_wait, 0)

        n_tiles = xs_hbm.shape[0] // TM
        n_tail = n_tiles - TOP_K * n_tok_tiles

        def tail_copy(i):
            row0 = pl.multiple_of((nv_ref[0] + i) * TM, TM)
            return pltpu.make_async_copy(zero_ref, xs_hbm.at[pl.ds(row0, TM)], sems.at[1])

        def tail_start(i, carry):
            @pl.when(nv_ref[0] + i < n_tiles)
            def _():
                tail_copy(i).start()
            return carry

        def tail_wait(i, carry):
            @pl.when(nv_ref[0] + i < n_tiles)
            def _():
                tail_copy(i).wait()
            return carry

        lax.fori_loop(0, n_tail, tail_start, 0)
        lax.fori_loop(0, n_tail, tail_wait, 0)


def _moe_dispatch(a, pos, pad_rows, n_valid, n_rows):
    nb, n, d = a.shape
    nt = n // TM
    kern = functools.partial(_dispatch_kernel, n_tok_tiles=nb * nt)
    return pl.pallas_call(
        kern,
        out_shape=jax.ShapeDtypeStruct((n_rows, d), a.dtype),
        grid_spec=pltpu.PrefetchScalarGridSpec(
            num_scalar_prefetch=3, grid=(nb * nt,),
            in_specs=[pl.BlockSpec((1, TM, d), lambda i, p, q, v: (i // nt, i % nt, 0))],
            out_specs=pl.BlockSpec(memory_space=pl.ANY),
            scratch_shapes=[pltpu.VMEM((TM, d), a.dtype), pltpu.SemaphoreType.DMA((2,))]),
        compiler_params=_cp("arbitrary"),
        name="moe_dispatch",
    )(pos, pad_rows, n_valid, a)


def _experts_kernel(te_ref, xi_ref, nv_ref, x_ref, w1_ref, w3_ref, w2_ref, y_ref, hm_ref):
    @pl.when(pl.program_id(0) < nv_ref[0])
    def _():
        _swiglu_hidden(x_ref[...].astype(BF16), w1_ref.at[0], w3_ref.at[0], hm_ref)
        y_ref[...] = _dot(hm_ref[...], w2_ref[0])

    @pl.when(pl.program_id(0) >= nv_ref[0])
    def _():
        y_ref[...] = jnp.zeros_like(y_ref)


def _moe_experts(xs, tile_expert, x_index, n_valid, w1, w3, w2, j, n_tiles):
    dp = xs.shape[-1]
    d, f = w1.shape[-2:]
    rows = pl.BlockSpec((TM, dp), lambda i, te, xi, nv: (xi[i], 0))
    wspec = lambda s: pl.BlockSpec((None, 1) + s, lambda i, te, xi, nv: (j, te[i], 0, 0))
    return pl.pallas_call(
        _experts_kernel,
        out_shape=jax.ShapeDtypeStruct((n_tiles * TM, dp), xs.dtype),
        grid_spec=pltpu.PrefetchScalarGridSpec(
            num_scalar_prefetch=3, grid=(n_tiles,),
            in_specs=[rows, wspec((d, f)), wspec((d, f)), wspec((f, d))],
            out_specs=pl.BlockSpec((TM, dp), lambda i, te, xi, nv: (i, 0)),
            scratch_shapes=[pltpu.VMEM((TM, f), BF16)]),
        compiler_params=_cp("arbitrary"),
        name="moe_experts",
    )(tile_expert, x_index, n_valid, xs, w1, w3, w2)


def _combine_kernel(pos_ref, h_ref, mod_ref, rw_ref, ys_hbm, o_ref, y1_ref, y2_ref, sems, *, nt, fg_ref=None):
    base = (pl.program_id(0) * nt + pl.program_id(1)) * TM

    for r in range(TM):
        t = base + r
        _row_copy(ys_hbm, pos_ref[2 * t], y1_ref, r, sems.at[0]).start()
        _row_copy(ys_hbm, pos_ref[2 * t + 1], y2_ref, r, sems.at[1]).start()
    pltpu.make_async_copy(ys_hbm.at[pl.ds(0, TM)], y1_ref, sems.at[0]).wait()
    pltpu.make_async_copy(ys_hbm.at[pl.ds(0, TM)], y2_ref, sems.at[1]).wait()
    rw = rw_ref[0]
    y = rw[:, 0:1] * y1_ref[...] + rw[:, 1:2] * y2_ref[...]
    out = h_ref[0] + mod_ref[0][5:6] * y
    if fg_ref is not None:
        out = out * lax.rsqrt(jnp.mean(out * out, axis=-1, keepdims=True) + EPS) * fg_ref[...]
    o_ref[0] = out


def _combine_final_kernel(pos_ref, h_ref, mod_ref, rw_ref, fg_ref, ys_hbm, o_ref, y1_ref, y2_ref, sems, *, nt):
    _combine_kernel(pos_ref, h_ref, mod_ref, rw_ref, ys_hbm, o_ref, y1_ref, y2_ref, sems, nt=nt, fg_ref=fg_ref)


def _moe_combine(ys, pos, rw, h, mods, ctx_tiles, t0, final_g=None):
    nb, l, d = h.shape
    nt = l // TM - t0
    tok = lambda width, off: pl.BlockSpec((1, TM, width), lambda b, t, p: (b, t + off, 0))
    mod = pl.BlockSpec((1, 6, d), lambda b, t, p: (jnp.where(t + t0 < ctx_tiles, nb, b), 0, 0))
    scratch = [pltpu.VMEM((TM, ys.shape[-1]), ys.dtype), pltpu.VMEM((TM, ys.shape[-1]), ys.dtype),
               pltpu.SemaphoreType.DMA((2,))]
    any_spec = pl.BlockSpec(memory_space=pl.ANY)
    if final_g is None:
        return pl.pallas_call(
            functools.partial(_combine_kernel, nt=nt),
            out_shape=jax.ShapeDtypeStruct((nb, l, d), F32),
            grid_spec=pltpu.PrefetchScalarGridSpec(
                num_scalar_prefetch=1, grid=(nb, nt),
                in_specs=[tok(d, t0), mod, tok(LANES, 0), any_spec],
                out_specs=tok(d, t0), scratch_shapes=scratch),
            input_output_aliases={1: 0},
            compiler_params=_cp("arbitrary", "arbitrary"),
            name="moe_combine",
        )(pos, h, mods, rw, ys)
    return pl.pallas_call(
        functools.partial(_combine_final_kernel, nt=nt),
        out_shape=jax.ShapeDtypeStruct((nb, nt * TM, d), F32),
        grid_spec=pltpu.PrefetchScalarGridSpec(
            num_scalar_prefetch=1, grid=(nb, nt),
            in_specs=[tok(d, t0), mod, tok(LANES, 0), pl.BlockSpec((1, d), lambda b, t, p: (0, 0)), any_spec],
            out_specs=tok(d, 0), scratch_shapes=scratch),
        compiler_params=_cp("arbitrary", "arbitrary"),
        name="moe_combine_final",
    )(pos, h, mods, rw, final_g.reshape(1, d), ys)


def _moe(h, mods, g, router, w1, w3, w2, j, ctx_tiles, t0, final_g=None):
    n_experts = router.shape[-1]
    a, ri, rw, counts = _moe_route(h, mods, g, router, ctx_tiles, t0)
    nb, n, d = a.shape
    tokens = nb * n
    n_tiles = (TOP_K * tokens) // TM + n_experts
    counts = counts[0, :n_experts]
    padded = ((counts + TM - 1) // TM) * TM
    ends = jnp.cumsum(padded)
    starts = ends - padded
    ri = ri.reshape(tokens, LANES)
    pos = (starts[ri[:, 0:TOP_K]] + ri[:, TOP_K:2 * TOP_K]).reshape(TOP_K * tokens)
    n_valid = ends[-1] // TM
    x_index = jnp.maximum(jnp.minimum(jnp.arange(n_tiles, dtype=jnp.int32), n_valid - 1), 0)
    tile_expert = jnp.minimum(jnp.sum(ends[None, :] <= (x_index * TM)[:, None], axis=1), n_experts - 1)
    fill = jnp.arange(TM, dtype=jnp.int32)[None, :]
    pad_rows = jnp.where(fill < (padded - counts)[:, None], (starts + counts)[:, None] + fill, -1)
    n_valid = n_valid.reshape(1).astype(jnp.int32)
    xs = _moe_dispatch(a, pos.astype(jnp.int32),
                       pad_rows.reshape(-1).astype(jnp.int32), n_valid, n_tiles * TM)
    ys = _moe_experts(xs, tile_expert.astype(jnp.int32), x_index, n_valid, w1, w3, w2, j, n_tiles)
    return _moe_combine(ys, pos.astype(jnp.int32), rw, h, mods, ctx_tiles, t0, final_g)


def kernel(x, c, ctx, c_ctx, ada_w, ada_b, norm_mix_g, norm_ffn_g, final_g, conv_pw1_w, conv_pw1_b, conv_dw_w, conv_dw_b, conv_ln_g, conv_ln_b, conv_pw2_w, conv_pw2_b, rec_w_in, rec_lb_logits, rec_onorm_g, rec_w_o, ffn_w1, ffn_w3, ffn_w2, moe_router, moe_w1, moe_w3, moe_w2):
    nb, seq, d = x.shape
    n_ctx = ctx.shape[1]
    depth = ada_w.shape[0]
    ctx_tiles = n_ctx // TM
    assert n_ctx % TM == 0 and seq % TM == 0 and n_ctx % BLK == 0 and seq % (GRID_W * 8) == 0
    assert d % (2 * 256) == 0 and nb < 16 and depth % 2 == 0

    cond = jnp.zeros((16, d), F32).at[:nb].set(c).at[nb].set(c_ctx)
    mods = _modulation(cond, ada_w, ada_b)
    sm = jax.nn.softmax(rec_lb_logits.astype(F32), axis=1)
    lower = jnp.cumsum(sm, axis=1) - sm[:, :1]

    bf = lambda w: w.astype(BF16)
    conv_pw1_w, conv_pw2_w, rec_w_in, rec_w_o = bf(conv_pw1_w), bf(conv_pw2_w), bf(rec_w_in), bf(rec_w_o)
    ffn_w1, ffn_w3, ffn_w2 = bf(ffn_w1), bf(ffn_w3), bf(ffn_w2)
    moe_w1, moe_w3, moe_w2 = bf(moe_w1), bf(moe_w3), bf(moe_w2)

    h = jnp.concatenate([ctx, x], axis=1)
    for i in range(depth):
        last = i == depth - 1
        j = i // 2
        t0 = ctx_tiles if last else 0
        if i % 2 == 0:
            u = _pw1_glu(h, mods[i], norm_mix_g[i], conv_pw1_w, j, conv_pw1_b[j], ctx_tiles)
            v = _dwconv(u, conv_dw_w[j], conv_dw_b[j], n_ctx)
            h = _pw2_residual(v, h, mods[i], conv_ln_g[j], conv_ln_b[j], conv_pw2_w, j, conv_pw2_b[j],
                              ctx_tiles, t0)
            h = _ffn_dense(h, mods[i], norm_ffn_g[i], ffn_w1, ffn_w3, ffn_w2, j, ctx_tiles)
        else:
            p = _rec_in_proj(h, mods[i], norm_mix_g[i], rec_w_in, j, lower[:, i], ctx_tiles)
            o_f, o_b = _gla(p, n_ctx)
            h = _rec_readout(o_f, o_b, p, h, mods[i], rec_onorm_g[j], rec_w_o, j, ctx_tiles, t0)
            h = _moe(h, mods[i], norm_ffn_g[i], moe_router[j], moe_w1, moe_w3, moe_w2, j, ctx_tiles, t0,
                     final_g if last else None)
    return h
```

```python
import functools

import numpy as np
import jax
import jax.numpy as jnp
from jax import lax
from jax.experimental import pallas as pl
from jax.experimental.pallas import tpu as pltpu

EPS = 1e-6
GRID_W = 64
CONV_PAD = 16
HEAD_DIM = 128
BLK = 128
CHUNK = 16
TOP_K = 2
TM = 256
LANES = 128
VMEM_LIMIT = 56 * 1024 * 1024

F32 = jnp.float32
BF16 = jnp.bfloat16


def _cp(*sem):
    return pltpu.CompilerParams(dimension_semantics=sem, vmem_limit_bytes=VMEM_LIMIT)


def _dot(a, b):
    return jnp.dot(a, b, preferred_element_type=F32)


def _dot_nt(a, b):
    return lax.dot_general(a, b, (((1,), (1,)), ((), ())), preferred_element_type=F32)


def _dot_tn(a, b):
    return lax.dot_general(a, b, (((0,), (0,)), ((), ())), preferred_element_type=F32)


def _sigmoid(x):
    return 1.0 / (1.0 + jnp.exp(-x))


def _silu(x):
    return x * _sigmoid(x)


def _rms_mod(x, g, shift, scale):
    ms = jnp.mean(x * x, axis=-1, keepdims=True)
    return (x * lax.rsqrt(ms + EPS) * g) * (1.0 + scale) + shift


def _mod_kernel(cond_ref, w_ref, b_ref, o_ref):
    s = _silu(cond_ref[...])
    o_ref[0] = jnp.dot(s, w_ref[0], preferred_element_type=F32,
                       precision=lax.Precision.HIGHEST) + b_ref[0]


def _modulation(cond, ada_w, ada_b):
    depth, d, n = ada_w.shape
    tn = n // 4
    out = pl.pallas_call(
        _mod_kernel,
        out_shape=jax.ShapeDtypeStruct((depth, cond.shape[0], n), F32),
        grid=(depth, n // tn),
        in_specs=[
            pl.BlockSpec(cond.shape, lambda l, j: (0, 0)),
            pl.BlockSpec((1, d, tn), lambda l, j: (l, 0, j)),
            pl.BlockSpec((1, 1, tn), lambda l, j: (l, 0, j)),
        ],
        out_specs=pl.BlockSpec((1, cond.shape[0], tn), lambda l, j: (l, 0, j)),
        compiler_params=_cp("arbitrary", "arbitrary"),
        name="modulation",
    )(cond, ada_w, ada_b.reshape(depth, 1, n))
    return out.reshape(depth, cond.shape[0], 6, d)


def _row_specs(nb, d, t0, ctx_tiles):
    tok = lambda width: pl.BlockSpec((1, TM, width), lambda b, t: (b, t + t0, 0))
    mod = pl.BlockSpec((1, 6, d), lambda b, t: (jnp.where(t + t0 < ctx_tiles, nb, b), 0, 0))
    return tok, mod


def _full(shape):
    return pl.BlockSpec(shape, lambda b, t: (0,) * len(shape))


def _layer(shape, j):
    return pl.BlockSpec((None,) + shape, lambda b, t: (j,) + (0,) * len(shape))


def _pw1_kernel(x_ref, mod_ref, g_ref, w_ref, b_ref, u_ref):
    m = mod_ref[0]
    a = _rms_mod(x_ref[0], g_ref[...], m[0:1], m[1:2]).astype(BF16)
    d = u_ref.shape[-1]
    cw = 512
    for c in range(0, d, cw):
        a1 = _dot(a, w_ref[:, c:c + cw]) + b_ref[:, c:c + cw]
        a2 = _dot(a, w_ref[:, d + c:d + c + cw]) + b_ref[:, d + c:d + c + cw]
        u_ref[0, :, c:c + cw] = a1 * _sigmoid(a2)


def _pw1_glu(h, mods, g, w, j, b, ctx_tiles):
    nb, l, d = h.shape
    tok, mod = _row_specs(nb, d, 0, ctx_tiles)
    return pl.pallas_call(
        _pw1_kernel,
        out_shape=jax.ShapeDtypeStruct((nb, l, d), F32),
        grid=(nb, l // TM),
        in_specs=[tok(d), mod, _full((1, d)), _layer((d, 2 * d), j), _full((1, 2 * d))],
        out_specs=tok(d),
        compiler_params=_cp("parallel", "arbitrary"),
        name="conv_pw1_glu",
    )(h, mods, g.reshape(1, d), w, b.reshape(1, 2 * d))


def _conv_kernel(u_ref, w_ref, b_ref, v_ref, pad_ref, *, ctx, rows, width_tiles):
    kw = w_ref.shape[0]
    half = (kw - 1) // 2
    cw = u_ref.shape[-1]
    j = pl.program_id(1)
    bias = jnp.broadcast_to(b_ref[...], (GRID_W, cw))
    zeros = jnp.zeros((CONV_PAD, cw), F32)

    def line_taps(win):
        acc = bias
        n = win.shape[0]
        shifted = {0: win}
        for k in range(kw):
            s = CONV_PAD + k - half
            b, a8 = s % 8, (s // 8) * 8
            if b not in shifted:
                shifted[b] = pltpu.roll(win, n - b, axis=0)
            acc = acc + w_ref[k:k + 1, :] * shifted[b][a8:a8 + GRID_W]
        return acc

    pad_ref[0:CONV_PAD] = zeros
    pad_ref[CONV_PAD:CONV_PAD + ctx] = u_ref[0, 0:ctx]
    pad_ref[CONV_PAD + ctx:2 * CONV_PAD + ctx] = zeros
    for c0 in range(0, ctx, GRID_W):
        v_ref[0, c0:c0 + GRID_W] = line_taps(pad_ref[c0:c0 + GRID_W + 2 * CONV_PAD])

    @pl.when(j < width_tiles)
    def _along_width():
        stride = GRID_W + CONV_PAD
        pad_ref[0:CONV_PAD] = zeros
        for r in range(rows):
            base = CONV_PAD + r * stride
            pad_ref[base:base + GRID_W] = u_ref[0, ctx + r * GRID_W:ctx + (r + 1) * GRID_W]
            pad_ref[base + GRID_W:base + stride] = zeros

        def body(r, carry):
            base = pl.multiple_of(r * stride, 8)
            win = pad_ref[pl.ds(base, GRID_W + 2 * CONV_PAD)]
            out0 = pl.multiple_of(ctx + r * GRID_W, 8)
            v_ref[0, pl.ds(out0, GRID_W)] = line_taps(win)
            return carry

        lax.fori_loop(0, rows, body, 0)

    @pl.when(j >= width_tiles)
    def _along_height():
        npad = half * GRID_W
        zrow = jnp.zeros((GRID_W, cw), F32)
        for r in range(half):
            pad_ref[r * GRID_W:(r + 1) * GRID_W] = zrow
            lo = npad + (rows + r) * GRID_W
            pad_ref[lo:lo + GRID_W] = zrow
        pad_ref[npad:npad + rows * GRID_W] = u_ref[0, ctx:ctx + rows * GRID_W]

        def body(r, carry):
            acc = bias
            for k in range(kw):
                src = pl.multiple_of((r + k) * GRID_W, 8)
                acc = acc + w_ref[k:k + 1, :] * pad_ref[pl.ds(src, GRID_W)]
            out0 = pl.multiple_of(ctx + r * GRID_W, 8)
            v_ref[0, pl.ds(out0, GRID_W)] = acc
            return carry

        lax.fori_loop(0, rows, body, 0)


def _dwconv(u, w, b, ctx):
    nb, l, d = u.shape
    kw = w.shape[0]
    cw = 256
    rows = (l - ctx) // GRID_W
    half = (kw - 1) // 2
    pad_rows = max(2 * CONV_PAD + ctx, CONV_PAD + rows * (GRID_W + CONV_PAD),
                   (rows + 2 * half) * GRID_W)
    kern = functools.partial(_conv_kernel, ctx=ctx, rows=rows, width_tiles=(d // 2) // cw)
    return pl.pallas_call(
        kern,
        out_shape=jax.ShapeDtypeStruct((nb, l, d), F32),
        grid=(nb, d // cw),
        in_specs=[
            pl.BlockSpec((1, l, cw), lambda b_, j: (b_, 0, j)),
            pl.BlockSpec((kw, cw), lambda b_, j: (0, j)),
            pl.BlockSpec((1, cw), lambda b_, j: (0, j)),
        ],
        out_specs=pl.BlockSpec((1, l, cw), lambda b_, j: (b_, 0, j)),
        scratch_shapes=[pltpu.VMEM((pad_rows, cw), F32)],
        compiler_params=_cp("parallel", "arbitrary"),
        name="axial_dwconv",
    )(u, w, b.reshape(1, d))


def _swiglu_hidden(a, w1_ref, w3_ref, hm_ref):
    f = hm_ref.shape[-1]
    cw = 256
    for c in range(0, f, cw):
        h1 = _dot(a, w1_ref[:, c:c + cw])
        h3 = _dot(a, w3_ref[:, c:c + cw])
        hm_ref[:, c:c + cw] = (_silu(h1) * h3).astype(BF16)


def _pw2_ffn_kernel(v_ref, h_ref, mod_ref, lng_ref, lnb_ref, pw_ref, pb_ref, g_ref, w1_ref, w3_ref, w2_ref,
                    o_ref, hm_ref):
    m = mod_ref[0]
    v = v_ref[0]
    mu = jnp.mean(v, axis=-1, keepdims=True)
    vc = v - mu
    var = jnp.mean(vc * vc, axis=-1, keepdims=True)
    y = _silu(vc * lax.rsqrt(var + EPS) * lng_ref[...] + lnb_ref[...]).astype(BF16)
    h = h_ref[0] + m[2:3] * (_dot(y, pw_ref[...]) + pb_ref[...])
    a = _rms_mod(h, g_ref[...], m[3:4], m[4:5]).astype(BF16)
    _swiglu_hidden(a, w1_ref, w3_ref, hm_ref)
    o_ref[0] = h + m[5:6] * _dot(hm_ref[...], w2_ref[...])


def _pw2_ffn(v, h, mods, ln_g, ln_b, pw, j, pb, g, w1, w3, w2, ctx_tiles):
    nb, l, d = h.shape
    f = w1.shape[-1]
    tok, mod = _row_specs(nb, d, 0, ctx_tiles)
    return pl.pallas_call(
        _pw2_ffn_kernel,
        out_shape=jax.ShapeDtypeStruct((nb, l, d), F32),
        grid=(nb, l // TM),
        in_specs=[tok(d), tok(d), mod, _full((1, d)), _full((1, d)), _layer((d, d), j), _full((1, d)),
                  _full((1, d)), _layer((d, f), j), _layer((d, f), j), _layer((f, d), j)],
        out_specs=tok(d),
        scratch_shapes=[pltpu.VMEM((TM, f), BF16)],
        input_output_aliases={1: 0},
        compiler_params=_cp("parallel", "arbitrary"),
        name="conv_pw2_ffn",
    )(v, h, mods, ln_g.reshape(1, d), ln_b.reshape(1, d), pw, pb.reshape(1, d), g.reshape(1, d), w1, w3, w2)


def _win_kernel(x_ref, mod_ref, g_ref, w_ref, lb_ref, p_ref):
    m = mod_ref[0]
    a = _rms_mod(x_ref[0], g_ref[...], m[0:1], m[1:2]).astype(BF16)
    d = x_ref.shape[-1]
    cw = 512
    for c in range(0, 5 * d, cw):
        part, off = divmod(c, d)
        y = _dot(a, w_ref[:, c:c + cw])
        if part in (1, 2):
            lb = lb_ref[part - 1:part, off:off + cw]
            y = jnp.log(lb + (1.0 - lb) * _sigmoid(y))
        elif part == 3:
            y = _silu(y) * (HEAD_DIM ** -0.5)
        elif part == 4:
            y = _silu(y)
        p_ref[0, :, c:c + cw] = y


def _rec_in_proj(h, mods, g, w, j, lower, ctx_tiles):
    nb, l, d = h.shape
    n = w.shape[-1]
    tok, mod = _row_specs(nb, d, 0, ctx_tiles)
    return pl.pallas_call(
        _win_kernel,
        out_shape=jax.ShapeDtypeStruct((nb, l, n), F32),
        grid=(nb, l // TM),
        in_specs=[tok(d), mod, _full((1, d)), _layer((d, n), j), _full((2, d))],
        out_specs=tok(n),
        compiler_params=_cp("parallel", "arbitrary"),
        name="rec_in_proj",
    )(h, mods, g.reshape(1, d), w, lower)


N_CHUNKS = BLK // CHUNK
N_CHUNK_SUMS = 7
SUM_ROWS = BLK + N_CHUNK_SUMS * N_CHUNKS + 8


def _rec_constants():
    t = np.arange(BLK)
    mats, masks = [], []
    for direction in range(2):
        p = t if direction == 0 else BLK - 1 - t
        pt, ps = p[:, None], p[None, :]
        ct, cs = pt // CHUNK, ps // CHUNK
        same = ct == cs
        mid4 = N_CHUNKS // 2
        mid2 = (ct // 4) * 4 + 2
        chunk_level = [
            same,
            cs < ct,
            cs > ct,
            (ct >= mid4) & (cs >= mid4) & (cs < ct),
            (ct < mid4) & (cs > ct) & (cs < mid4),
            (ct >= mid2) & (cs >= mid2) & (cs < ct),
            (ct < mid2) & (cs > ct) & (cs < mid2),
        ]
        rows = [same & (ps <= pt)]
        for m in chunk_level:
            m = np.broadcast_to(m, (BLK, BLK))
            rows.append(m[::CHUNK])
        rows.append(np.ones((8, BLK), bool))
        mats.append(np.concatenate([r.astype(np.float32) for r in rows], axis=0))
        m = [
            same & (ps <= pt),
            (ct % 2 == 1) & (cs == ct - 1),
            (ct // 4 == cs // 4) & (ct % 4 >= 2) & (cs % 4 < 2),
            (ct >= mid4) & (cs < mid4),
        ]
        level = np.full((BLK, BLK), float(len(m)), np.float32)
        for i, x in enumerate(m):
            level = np.where(np.broadcast_to(x, (BLK, BLK)), np.float32(i), level)
        masks.append(level)
    return np.stack(mats), np.stack(masks)


def _chunk_scale(x, ec, i):
    base = i * N_CHUNKS
    parts = []
    for j in range(N_CHUNKS):
        e = jnp.broadcast_to(ec[base + j:base + j + 1], (CHUNK, x.shape[1]))
        parts.append(x[CHUNK * j:CHUNK * (j + 1)] * e)
    return jnp.concatenate(parts, axis=0)


def _gla_kernel(v_ref, lff_ref, lfb_ref, q_ref, a_ref, m_ref, of_ref, ob_ref, *, ctx_blocks, n_blocks):
    last = n_blocks - 1
    n_heads = v_ref.shape[-1] // HEAD_DIM
    chains = [(dr, hd) for dr in range(2) for hd in range(n_heads)]
    lf_refs = (lff_ref, lfb_ref)
    o_refs = (of_ref, ob_ref)
    tot_row = N_CHUNK_SUMS * N_CHUNKS

    def block_rows(n):
        rev = jnp.where(n < ctx_blocks, ctx_blocks - 1 - n, last + ctx_blocks - n)
        return pl.multiple_of(n * BLK, BLK), pl.multiple_of(rev * BLK, BLK)

    def decay_sums(row0):
        sums = []
        for dr in range(2):
            logf = lf_refs[dr][0, pl.ds(row0[dr], BLK), :]
            hi = logf.astype(BF16)
            lo = (logf - hi.astype(F32)).astype(BF16)
            sums.append((logf, _dot(a_ref[dr], jnp.concatenate([hi, lo], axis=0))))
        return sums

    def operands(row0, sums):
        ops = []
        for dr, hd in chains:
            cols = slice(hd * HEAD_DIM, (hd + 1) * HEAD_DIM)
            logf, s = sums[dr]
            s = s[:, cols]
            k = 1.0 - jnp.exp(logf[:, cols])
            qs = q_ref[0, pl.ds(row0[dr], BLK), cols]
            bcum = s[0:BLK]
            ec = jnp.exp(s[BLK:SUM_ROWS])
            q_in = qs * jnp.exp(bcum)
            k_in = k * jnp.exp(-bcum)
            k_st = _chunk_scale(k_in, ec, 0)
            ops.append(dict(
                q_blk=_chunk_scale(q_in, ec, 1).astype(BF16),
                k_end=_chunk_scale(k_st, ec, 2).astype(BF16),
                q4=_chunk_scale(q_in, ec, 3).astype(BF16),
                k4=_chunk_scale(k_st, ec, 4).astype(BF16),
                q2=_chunk_scale(q_in, ec, 5).astype(BF16),
                k2=_chunk_scale(k_st, ec, 6).astype(BF16),
                e_tot=ec[tot_row:tot_row + 1],
                q_in=q_in.astype(BF16), k_in=k_in.astype(BF16), k_st=k_st.astype(BF16),
                vb=v_ref[0, pl.ds(row0[dr], BLK), cols].astype(BF16)))
        return ops

    def products(ops, states):
        return [(_dot_nt(o["q_in"], o["k_in"]), _dot_nt(o["q_in"], o["k_st"]),
                 _dot_nt(o["q2"], o["k2"]), _dot_nt(o["q4"], o["k4"]),
                 _dot_nt(o["q_blk"], st.astype(BF16)), _dot_tn(o["vb"], o["k_end"]))
                for o, st in zip(ops, states)]

    def finish(row0, ops, raw, states):
        atts = []
        for (dr, hd), r in zip(chains, raw):
            level = m_ref[dr]
            att = jnp.where(level == 0.0, r[0], jnp.where(level == 1.0, r[1], jnp.where(
                level == 2.0, r[2], jnp.where(level == 3.0, r[3], 0.0))))
            atts.append(att.astype(BF16))
        new_states = []
        for (dr, hd), o, r, att, st in zip(chains, ops, raw, atts, states):
            cols = slice(hd * HEAD_DIM, (hd + 1) * HEAD_DIM)
            o_refs[dr][0, pl.ds(row0[dr], BLK), cols] = (_dot(att, o["vb"]) + r[4]).astype(o_refs[dr].dtype)
            new_states.append(o["e_tot"] * st + r[5])
        return tuple(new_states)

    def step(n, states):
        row0 = block_rows(n)
        ops = operands(row0, decay_sums(row0))
        return finish(row0, ops, products(ops, states), states)

    zero = jnp.zeros((HEAD_DIM, HEAD_DIM), F32)
    lax.fori_loop(0, n_blocks, step, (zero,) * len(chains))


GLA_HEADS_PER_STEP = 4


def _gla(p, ctx):
    nb, l, n5 = p.shape
    d = n5 // 5
    cw = GLA_HEADS_PER_STEP * HEAD_DIM
    groups = d // cw
    amat, masks = _rec_constants()
    amat = np.concatenate([amat, amat], axis=2)
    kern = functools.partial(_gla_kernel, ctx_blocks=ctx // BLK, n_blocks=l // BLK)
    col = lambda base: pl.BlockSpec((1, l, cw), lambda b, h: (b, 0, base + h))
    out = jax.ShapeDtypeStruct((nb, l, d), BF16)
    return pl.pallas_call(
        kern,
        out_shape=(out, out),
        grid=(nb, groups),
        in_specs=[col(0), col(groups), col(2 * groups), col(3 * groups),
                  _full(amat.shape), _full(masks.shape)],
        out_specs=(col(0), col(0)),
        compiler_params=_cp("parallel", "arbitrary"),
        name="hgrn2_gla",
    )(p, p, p, p, jnp.asarray(amat, BF16), jnp.asarray(masks, F32))


def _route_kernel(of_ref, ob_ref, gate_ref, h_ref, mod_ref, ng_ref, wo_ref, g_ref, r_ref, tri_ref,
                  hn_ref, a_ref, ri_ref, rw_ref, cnt_ref, carry_ref, y_ref, *, n_experts):
    @pl.when((pl.program_id(0) == 0) & (pl.program_id(1) == 0))
    def _():
        carry_ref[...] = jnp.zeros_like(carry_ref)

    m = mod_ref[0]
    d = of_ref.shape[-1]
    for c in range(0, d, HEAD_DIM):
        o = of_ref[0, :, c:c + HEAD_DIM].astype(F32) + ob_ref[0, :, c:c + HEAD_DIM].astype(F32)
        o = o * lax.rsqrt(jnp.mean(o * o, axis=-1, keepdims=True) + EPS) * ng_ref[:, c:c + HEAD_DIM]
        y_ref[:, c:c + HEAD_DIM] = (o * gate_ref[0, :, c:c + HEAD_DIM]).astype(BF16)
    h = h_ref[0] + m[2:3] * _dot(y_ref[...], wo_ref[...])
    hn_ref[0] = h
    a = _rms_mod(h, g_ref[...], m[3:4], m[4:5])
    a_ref[0] = a
    logits = jnp.dot(a, r_ref[...], preferred_element_type=F32, precision=lax.Precision.HIGHEST)
    lane = lax.broadcasted_iota(jnp.int32, logits.shape, 1)
    neg = jnp.float32(-jnp.inf)
    logits = jnp.where(lane < n_experts, logits, neg)
    v1 = jnp.max(logits, axis=-1, keepdims=True)
    i1 = jnp.min(jnp.where(logits == v1, lane, LANES), axis=-1, keepdims=True)
    rest = jnp.where(lane == i1, neg, logits)
    v2 = jnp.max(rest, axis=-1, keepdims=True)
    i2 = jnp.min(jnp.where(rest == v2, lane, LANES), axis=-1, keepdims=True)
    e2 = jnp.exp(v2 - v1)
    w1 = 1.0 / (1.0 + e2)
    w2 = e2 / (1.0 + e2)
    oh1 = jnp.where(lane == i1, 1.0, 0.0)
    oh2 = jnp.where(lane == i2, 1.0, 0.0)
    both = oh1 + oh2
    earlier = _dot(tri_ref[...], both.astype(BF16)) + carry_ref[...]
    r1 = jnp.sum(oh1 * earlier, axis=-1, keepdims=True).astype(jnp.int32)
    r2 = jnp.sum(oh2 * earlier, axis=-1, keepdims=True).astype(jnp.int32)
    carry = carry_ref[...] + jnp.sum(both, axis=0, keepdims=True)
    carry_ref[...] = carry
    cnt_ref[...] = jnp.broadcast_to(carry, cnt_ref.shape).astype(jnp.int32)
    ri_ref[0] = jnp.where(lane == 0, i1, jnp.where(lane == 1, i2, jnp.where(lane == 2, r1,
                          jnp.where(lane == 3, r2, 0))))
    rw_ref[0] = jnp.where(lane == 0, w1, jnp.where(lane == 1, w2, 0.0))


def _readout_route(o_f, o_b, p, h, mods, onorm_g, w_o, j, g, router, ctx_tiles, t0):
    nb, l, d = h.shape
    n_experts = router.shape[-1]
    nt = l // TM - t0
    tok, mod = _row_specs(nb, d, t0, ctx_tiles)
    gate_cols = pl.BlockSpec((1, TM, d), lambda b, t: (b, t + t0, 4))
    out_tok = lambda width: pl.BlockSpec((1, TM, width), lambda b, t: (b, t, 0))
    rpad = jnp.pad(router, ((0, 0), (0, LANES - n_experts)))
    tri = jnp.asarray(np.tril(np.ones((TM, TM), np.float32), -1), BF16)
    return pl.pallas_call(
        functools.partial(_route_kernel, n_experts=n_experts),
        out_shape=(jax.ShapeDtypeStruct((nb, l, d), F32),
                   jax.ShapeDtypeStruct((nb, nt * TM, d), F32),
                   jax.ShapeDtypeStruct((nb, nt * TM, LANES), jnp.int32),
                   jax.ShapeDtypeStruct((nb, nt * TM, LANES), F32),
                   jax.ShapeDtypeStruct((8, LANES), jnp.int32)),
        grid=(nb, nt),
        in_specs=[tok(d), tok(d), gate_cols, tok(d), mod, _full((1, d)), _layer((d, d), j),
                  _full((1, d)), _full((d, LANES)), _full((TM, TM))],
        out_specs=(tok(d), out_tok(d), out_tok(LANES), out_tok(LANES), _full((8, LANES))),
        scratch_shapes=[pltpu.VMEM((1, LANES), F32), pltpu.VMEM((TM, d), BF16)],
        input_output_aliases={3: 0},
        compiler_params=_cp("arbitrary", "arbitrary"),
        name="rec_readout_route",
    )(o_f, o_b, p, h, mods, onorm_g.reshape(1, d), w_o, g.reshape(1, d), rpad, tri)


def _row_copy(src, i, dst, j, sem):
    return pltpu.make_async_copy(src.at[pl.ds(i, 1)], dst.at[pl.ds(j, 1)], sem)


def _dispatch_kernel(pos_ref, pad_ref, nv_ref, a_ref, xs_hbm, zero_ref, sems, *, n_tok_tiles):
    step = pl.program_id(0)
    base = step * TM
    tile = a_ref.at[0]

    for r in range(TM):
        t = base + r
        _row_copy(tile, r, xs_hbm, pos_ref[2 * t], sems.at[0]).start(priority=0)
        _row_copy(tile, r, xs_hbm, pos_ref[2 * t + 1], sems.at[0]).start(priority=1)
    for _ in range(TOP_K):
        pltpu.make_async_copy(tile, xs_hbm.at[pl.ds(0, TM)], sems.at[0]).wait()

    @pl.when(step == n_tok_tiles - 1)
    def _fill():
        zero_ref[...] = jnp.zeros_like(zero_ref)

        def pad_start(i, carry):
            @pl.when(pad_ref[i] >= 0)
            def _():
                _row_copy(zero_ref, 0, xs_hbm, pad_ref[i], sems.at[1]).start()
            return carry

        def pad_wait(i, carry):
            @pl.when(pad_ref[i] >= 0)
            def _():
                _row_copy(zero_ref, 0, xs_hbm, 0, sems.at[1]).wait()
            return carry

        lax.fori_loop(0, pad_ref.shape[0], pad_start, 0)
        lax.fori_loop(0, pad_ref.shape[0], pad_wait, 0)

        n_tiles = xs_hbm.shape[0] // TM
        n_tail = n_tiles - TOP_K * n_tok_tiles

        def tail_copy(i):
            row0 = pl.multiple_of((nv_ref[0] + i) * TM, TM)
            return pltpu.make_async_copy(zero_ref, xs_hbm.at[pl.ds(row0, TM)], sems.at[1])

        def tail_start(i, carry):
            @pl.when(nv_ref[0] + i < n_tiles)
            def _():
                tail_copy(i).start()
            return carry

        def tail_wait(i, carry):
            @pl.when(nv_ref[0] + i < n_tiles)
            def _():
                tail_copy(i).wait()
            return carry

        lax.fori_loop(0, n_tail, tail_start, 0)
        lax.fori_loop(0, n_tail, tail_wait, 0)


def _moe_dispatch(a, pos, pad_rows, n_valid, n_rows):
    nb, n, d = a.shape
    nt = n // TM
    kern = functools.partial(_dispatch_kernel, n_tok_tiles=nb * nt)
    return pl.pallas_call(
        kern,
        out_shape=jax.ShapeDtypeStruct((n_rows, d), a.dtype),
        grid_spec=pltpu.PrefetchScalarGridSpec(
            num_scalar_prefetch=3, grid=(nb * nt,),
            in_specs=[pl.BlockSpec((1, TM, d), lambda i, p, q, v: (i // nt, i % nt, 0))],
            out_specs=pl.BlockSpec(memory_space=pl.ANY),
            scratch_shapes=[pltpu.VMEM((TM, d), a.dtype), pltpu.SemaphoreType.DMA((2,))]),
        compiler_params=_cp("arbitrary"),
        name="moe_dispatch",
    )(pos, pad_rows, n_valid, a)


def _experts_kernel(te_ref, xi_ref, nv_ref, x_ref, w1_ref, w3_ref, w2_ref, y_ref, hm_ref):
    @pl.when(pl.program_id(0) < nv_ref[0])
    def _():
        _swiglu_hidden(x_ref[...].astype(BF16), w1_ref.at[0], w3_ref.at[0], hm_ref)
        y_ref[...] = _dot(hm_ref[...], w2_ref[0])

    @pl.when(pl.program_id(0) >= nv_ref[0])
    def _():
        y_ref[...] = jnp.zeros_like(y_ref)


def _moe_experts(xs, tile_expert, x_index, n_valid, w1, w3, w2, j, n_tiles):
    dp = xs.shape[-1]
    d, f = w1.shape[-2:]
    rows = pl.BlockSpec((TM, dp), lambda i, te, xi, nv: (xi[i], 0))
    wspec = lambda s: pl.BlockSpec((None, 1) + s, lambda i, te, xi, nv: (j, te[i], 0, 0))
    return pl.pallas_call(
        _experts_kernel,
        out_shape=jax.ShapeDtypeStruct((n_tiles * TM, dp), xs.dtype),
        grid_spec=pltpu.PrefetchScalarGridSpec(
            num_scalar_prefetch=3, grid=(n_tiles,),
            in_specs=[rows, wspec((d, f)), wspec((d, f)), wspec((f, d))],
            out_specs=pl.BlockSpec((TM, dp), lambda i, te, xi, nv: (i, 0)),
            scratch_shapes=[pltpu.VMEM((TM, f), BF16)]),
        compiler_params=_cp("arbitrary"),
        name="moe_experts",
    )(tile_expert, x_index, n_valid, xs, w1, w3, w2)


def _combine_kernel(pos_ref, h_ref, mod_ref, rw_ref, ys_hbm, o_ref, y1_ref, y2_ref, sems, *, nt, fg_ref=None):
    base = (pl.program_id(0) * nt + pl.program_id(1)) * TM

    for r in range(TM):
        t = base + r
        _row_copy(ys_hbm, pos_ref[2 * t], y1_ref, r, sems.at[0]).start(priority=0)
        _row_copy(ys_hbm, pos_ref[2 * t + 1], y2_ref, r, sems.at[1]).start(priority=1)
    pltpu.make_async_copy(ys_hbm.at[pl.ds(0, TM)], y1_ref, sems.at[0]).wait()
    pltpu.make_async_copy(ys_hbm.at[pl.ds(0, TM)], y2_ref, sems.at[1]).wait()
    rw = rw_ref[0]
    y = rw[:, 0:1] * y1_ref[...] + rw[:, 1:2] * y2_ref[...]
    out = h_ref[0] + mod_ref[0][5:6] * y
    if fg_ref is not None:
        out = out * lax.rsqrt(jnp.mean(out * out, axis=-1, keepdims=True) + EPS) * fg_ref[...]
    o_ref[0] = out


def _combine_final_kernel(pos_ref, h_ref, mod_ref, rw_ref, fg_ref, ys_hbm, o_ref, y1_ref, y2_ref, sems, *, nt):
    _combine_kernel(pos_ref, h_ref, mod_ref, rw_ref, ys_hbm, o_ref, y1_ref, y2_ref, sems, nt=nt, fg_ref=fg_ref)


def _moe_combine(ys, pos, rw, h, mods, ctx_tiles, t0, final_g=None):
    nb, l, d = h.shape
    nt = l // TM - t0
    tok = lambda width, off: pl.BlockSpec((1, TM, width), lambda b, t, p: (b, t + off, 0))
    mod = pl.BlockSpec((1, 6, d), lambda b, t, p: (jnp.where(t + t0 < ctx_tiles, nb, b), 0, 0))
    scratch = [pltpu.VMEM((TM, ys.shape[-1]), ys.dtype), pltpu.VMEM((TM, ys.shape[-1]), ys.dtype),
               pltpu.SemaphoreType.DMA((2,))]
    any_spec = pl.BlockSpec(memory_space=pl.ANY)
    if final_g is None:
        return pl.pallas_call(
            functools.partial(_combine_kernel, nt=nt),
            out_shape=jax.ShapeDtypeStruct((nb, l, d), F32),
            grid_spec=pltpu.PrefetchScalarGridSpec(
                num_scalar_prefetch=1, grid=(nb, nt),
                in_specs=[tok(d, t0), mod, tok(LANES, 0), any_spec],
                out_specs=tok(d, t0), scratch_shapes=scratch),
            input_output_aliases={1: 0},
            compiler_params=_cp("arbitrary", "arbitrary"),
            name="moe_combine",
        )(pos, h, mods, rw, ys)
    return pl.pallas_call(
        functools.partial(_combine_final_kernel, nt=nt),
        out_shape=jax.ShapeDtypeStruct((nb, nt * TM, d), F32),
        grid_spec=pltpu.PrefetchScalarGridSpec(
            num_scalar_prefetch=1, grid=(nb, nt),
            in_specs=[tok(d, t0), mod, tok(LANES, 0), pl.BlockSpec((1, d), lambda b, t, p: (0, 0)), any_spec],
            out_specs=tok(d, 0), scratch_shapes=scratch),
        compiler_params=_cp("arbitrary", "arbitrary"),
        name="moe_combine_final",
    )(pos, h, mods, rw, final_g.reshape(1, d), ys)


def _moe(h, a, ri, rw, counts, mods, n_experts, w1, w3, w2, j, ctx_tiles, t0, final_g=None):
    nb, n, d = a.shape
    tokens = nb * n
    n_tiles = (TOP_K * tokens) // TM + n_experts
    counts = counts[0, :n_experts]
    padded = ((counts + TM - 1) // TM) * TM
    ends = jnp.cumsum(padded)
    starts = ends - padded
    ri = ri.reshape(tokens, LANES)
    pos = (starts[ri[:, 0:TOP_K]] + ri[:, TOP_K:2 * TOP_K]).reshape(TOP_K * tokens)
    n_valid = ends[-1] // TM
    x_index = jnp.maximum(jnp.minimum(jnp.arange(n_tiles, dtype=jnp.int32), n_valid - 1), 0)
    tile_expert = jnp.minimum(jnp.sum(ends[None, :] <= (x_index * TM)[:, None], axis=1), n_experts - 1)
    fill = jnp.arange(TM, dtype=jnp.int32)[None, :]
    pad_rows = jnp.where(fill < (padded - counts)[:, None], (starts + counts)[:, None] + fill, -1)
    n_valid = n_valid.reshape(1).astype(jnp.int32)
    xs = _moe_dispatch(a, pos.astype(jnp.int32),
                       pad_rows.reshape(-1).astype(jnp.int32), n_valid, n_tiles * TM)
    ys = _moe_experts(xs, tile_expert.astype(jnp.int32), x_index, n_valid, w1, w3, w2, j, n_tiles)
    return _moe_combine(ys, pos.astype(jnp.int32), rw, h, mods, ctx_tiles, t0, final_g)


def kernel(x, c, ctx, c_ctx, ada_w, ada_b, norm_mix_g, norm_ffn_g, final_g, conv_pw1_w, conv_pw1_b, conv_dw_w, conv_dw_b, conv_ln_g, conv_ln_b, conv_pw2_w, conv_pw2_b, rec_w_in, rec_lb_logits, rec_onorm_g, rec_w_o, ffn_w1, ffn_w3, ffn_w2, moe_router, moe_w1, moe_w3, moe_w2):
    nb, seq, d = x.shape
    n_ctx = ctx.shape[1]
    depth = ada_w.shape[0]
    ctx_tiles = n_ctx // TM
    assert n_ctx % TM == 0 and seq % TM == 0 and n_ctx % BLK == 0 and seq % (GRID_W * 8) == 0
    assert d % (2 * 256) == 0 and nb < 16 and depth % 2 == 0

    cond = jnp.zeros((16, d), F32).at[:nb].set(c).at[nb].set(c_ctx)
    mods = _modulation(cond, ada_w, ada_b)
    sm = jax.nn.softmax(rec_lb_logits.astype(F32), axis=1)
    lower = jnp.cumsum(sm, axis=1) - sm[:, :1]

    bf = lambda w: w.astype(BF16)
    conv_pw1_w, conv_pw2_w, rec_w_in, rec_w_o = bf(conv_pw1_w), bf(conv_pw2_w), bf(rec_w_in), bf(rec_w_o)
    ffn_w1, ffn_w3, ffn_w2 = bf(ffn_w1), bf(ffn_w3), bf(ffn_w2)
    moe_w1, moe_w3, moe_w2 = bf(moe_w1), bf(moe_w3), bf(moe_w2)

    h = jnp.concatenate([ctx, x], axis=1)
    for i in range(depth):
        last = i == depth - 1
        j = i // 2
        t0 = ctx_tiles if last else 0
        if i % 2 == 0:
            u = _pw1_glu(h, mods[i], norm_mix_g[i], conv_pw1_w, j, conv_pw1_b[j], ctx_tiles)
            v = _dwconv(u, conv_dw_w[j], conv_dw_b[j], n_ctx)
            h = _pw2_ffn(v, h, mods[i], conv_ln_g[j], conv_ln_b[j], conv_pw2_w, j, conv_pw2_b[j],
                         norm_ffn_g[i], ffn_w1, ffn_w3, ffn_w2, ctx_tiles)
        else:
            p = _rec_in_proj(h, mods[i], norm_mix_g[i], rec_w_in, j, lower[:, i], ctx_tiles)
            o_f, o_b = _gla(p, n_ctx)
            h, a, ri, rw, counts = _readout_route(o_f, o_b, p, h, mods[i], rec_onorm_g[j], rec_w_o, j,
                                                  norm_ffn_g[i], moe_router[j], ctx_tiles, t0)
            h = _moe(h, a, ri, rw, counts, mods[i], moe_router.shape[-1], moe_w1, moe_w3, moe_w2, j,
                     ctx_tiles, t0, final_g if last else None)
    return h
```

```python
import functools

import numpy as np
import jax
import jax.numpy as jnp
from jax import lax
from jax.experimental import pallas as pl
from jax.experimental.pallas import tpu as pltpu

EPS = 1e-6
GRID_W = 64
CONV_PAD = 16
HEAD_DIM = 128
BLK = 128
CHUNK = 16
TOP_K = 2
TM = 256
LANES = 128
VMEM_LIMIT = 56 * 1024 * 1024

F32 = jnp.float32
BF16 = jnp.bfloat16


def _cp(*sem):
    return pltpu.CompilerParams(dimension_semantics=sem, vmem_limit_bytes=VMEM_LIMIT)


def _dot(a, b):
    return jnp.dot(a, b, preferred_element_type=F32)


def _dot_nt(a, b):
    return lax.dot_general(a, b, (((1,), (1,)), ((), ())), preferred_element_type=F32)


def _dot_tn(a, b):
    return lax.dot_general(a, b, (((0,), (0,)), ((), ())), preferred_element_type=F32)


def _sigmoid(x):
    return 1.0 / (1.0 + jnp.exp(-x))


def _silu(x):
    return x * _sigmoid(x)


def _rms_mod(x, g, shift, scale):
    ms = jnp.mean(x * x, axis=-1, keepdims=True)
    return (x * lax.rsqrt(ms + EPS) * g) * (1.0 + scale) + shift


def _mod_kernel(cond_ref, w_ref, b_ref, o_ref):
    s = _silu(cond_ref[...])
    o_ref[0] = jnp.dot(s, w_ref[0], preferred_element_type=F32,
                       precision=lax.Precision.HIGHEST) + b_ref[0]


def _modulation(cond, ada_w, ada_b):
    depth, d, n = ada_w.shape
    tn = n // 4
    out = pl.pallas_call(
        _mod_kernel,
        out_shape=jax.ShapeDtypeStruct((depth, cond.shape[0], n), F32),
        grid=(depth, n // tn),
        in_specs=[
            pl.BlockSpec(cond.shape, lambda l, j: (0, 0)),
            pl.BlockSpec((1, d, tn), lambda l, j: (l, 0, j)),
            pl.BlockSpec((1, 1, tn), lambda l, j: (l, 0, j)),
        ],
        out_specs=pl.BlockSpec((1, cond.shape[0], tn), lambda l, j: (l, 0, j)),
        compiler_params=_cp("arbitrary", "arbitrary"),
        name="modulation",
    )(cond, ada_w, ada_b.reshape(depth, 1, n))
    return out.reshape(depth, cond.shape[0], 6, d)


def _row_specs(nb, d, t0, ctx_tiles):
    tok = lambda width: pl.BlockSpec((1, TM, width), lambda b, t: (b, t + t0, 0))
    mod = pl.BlockSpec((1, 6, d), lambda b, t: (jnp.where(t + t0 < ctx_tiles, nb, b), 0, 0))
    return tok, mod


def _full(shape):
    return pl.BlockSpec(shape, lambda b, t: (0,) * len(shape))


def _layer(shape, j):
    return pl.BlockSpec((None,) + shape, lambda b, t: (j,) + (0,) * len(shape))


def _pw1_kernel(x_ref, mod_ref, g_ref, w_ref, b_ref, u_ref):
    m = mod_ref[0]
    a = _rms_mod(x_ref[0], g_ref[...], m[0:1], m[1:2]).astype(BF16)
    d = u_ref.shape[-1]
    cw = 512
    for c in range(0, d, cw):
        a1 = _dot(a, w_ref[:, c:c + cw]) + b_ref[:, c:c + cw]
        a2 = _dot(a, w_ref[:, d + c:d + c + cw]) + b_ref[:, d + c:d + c + cw]
        u_ref[0, :, c:c + cw] = a1 * _sigmoid(a2)


def _pw1_glu(h, mods, g, w, j, b, ctx_tiles):
    nb, l, d = h.shape
    tok, mod = _row_specs(nb, d, 0, ctx_tiles)
    return pl.pallas_call(
        _pw1_kernel,
        out_shape=jax.ShapeDtypeStruct((nb, l, d), F32),
        grid=(nb, l // TM),
        in_specs=[tok(d), mod, _full((1, d)), _layer((d, 2 * d), j), _full((1, 2 * d))],
        out_specs=tok(d),
        compiler_params=_cp("parallel", "arbitrary"),
        name="conv_pw1_glu",
    )(h, mods, g.reshape(1, d), w, b.reshape(1, 2 * d))


def _conv_kernel(u_ref, w_ref, b_ref, v_ref, pad_ref, *, ctx, rows, width_tiles):
    kw = w_ref.shape[0]
    half = (kw - 1) // 2
    cw = u_ref.shape[-1]
    j = pl.program_id(1)
    bias = jnp.broadcast_to(b_ref[...], (GRID_W, cw))
    zeros = jnp.zeros((CONV_PAD, cw), F32)

    def line_taps(win):
        acc = bias
        n = win.shape[0]
        shifted = {0: win}
        for k in range(kw):
            s = CONV_PAD + k - half
            b, a8 = s % 8, (s // 8) * 8
            if b not in shifted:
                shifted[b] = pltpu.roll(win, n - b, axis=0)
            acc = acc + w_ref[k:k + 1, :] * shifted[b][a8:a8 + GRID_W]
        return acc

    pad_ref[0:CONV_PAD] = zeros
    pad_ref[CONV_PAD:CONV_PAD + ctx] = u_ref[0, 0:ctx]
    pad_ref[CONV_PAD + ctx:2 * CONV_PAD + ctx] = zeros
    for c0 in range(0, ctx, GRID_W):
        v_ref[0, c0:c0 + GRID_W] = line_taps(pad_ref[c0:c0 + GRID_W + 2 * CONV_PAD])

    @pl.when(j < width_tiles)
    def _along_width():
        stride = GRID_W + CONV_PAD
        pad_ref[0:CONV_PAD] = zeros
        for r in range(rows):
            base = CONV_PAD + r * stride
            pad_ref[base:base + GRID_W] = u_ref[0, ctx + r * GRID_W:ctx + (r + 1) * GRID_W]
            pad_ref[base + GRID_W:base + stride] = zeros

        def body(r, carry):
            base = pl.multiple_of(r * stride, 8)
            win = pad_ref[pl.ds(base, GRID_W + 2 * CONV_PAD)]
            out0 = pl.multiple_of(ctx + r * GRID_W, 8)
            v_ref[0, pl.ds(out0, GRID_W)] = line_taps(win)
            return carry

        lax.fori_loop(0, rows, body, 0)

    @pl.when(j >= width_tiles)
    def _along_height():
        npad = half * GRID_W
        zrow = jnp.zeros((GRID_W, cw), F32)
        for r in range(half):
            pad_ref[r * GRID_W:(r + 1) * GRID_W] = zrow
            lo = npad + (rows + r) * GRID_W
            pad_ref[lo:lo + GRID_W] = zrow
        pad_ref[npad:npad + rows * GRID_W] = u_ref[0, ctx:ctx + rows * GRID_W]

        def body(r, carry):
            acc = bias
            for k in range(kw):
                src = pl.multiple_of((r + k) * GRID_W, 8)
                acc = acc + w_ref[k:k + 1, :] * pad_ref[pl.ds(src, GRID_W)]
            out0 = pl.multiple_of(ctx + r * GRID_W, 8)
            v_ref[0, pl.ds(out0, GRID_W)] = acc
            return carry

        lax.fori_loop(0, rows, body, 0)


def _dwconv(u, w, b, ctx):
    nb, l, d = u.shape
    kw = w.shape[0]
    cw = 256
    rows = (l - ctx) // GRID_W
    half = (kw - 1) // 2
    pad_rows = max(2 * CONV_PAD + ctx, CONV_PAD + rows * (GRID_W + CONV_PAD),
                   (rows + 2 * half) * GRID_W)
    kern = functools.partial(_conv_kernel, ctx=ctx, rows=rows, width_tiles=(d // 2) // cw)
    return pl.pallas_call(
        kern,
        out_shape=jax.ShapeDtypeStruct((nb, l, d), F32),
        grid=(nb, d // cw),
        in_specs=[
            pl.BlockSpec((1, l, cw), lambda b_, j: (b_, 0, j)),
            pl.BlockSpec((kw, cw), lambda b_, j: (0, j)),
            pl.BlockSpec((1, cw), lambda b_, j: (0, j)),
        ],
        out_specs=pl.BlockSpec((1, l, cw), lambda b_, j: (b_, 0, j)),
        scratch_shapes=[pltpu.VMEM((pad_rows, cw), F32)],
        compiler_params=_cp("parallel", "arbitrary"),
        name="axial_dwconv",
    )(u, w, b.reshape(1, d))


def _swiglu_hidden(a, w1_ref, w3_ref, hm_ref):
    f = hm_ref.shape[-1]
    cw = 256
    for c in range(0, f, cw):
        h1 = _dot(a, w1_ref[:, c:c + cw])
        h3 = _dot(a, w3_ref[:, c:c + cw])
        hm_ref[:, c:c + cw] = (_silu(h1) * h3).astype(BF16)


def _pw2_ffn_kernel(v_ref, h_ref, mod_ref, lng_ref, lnb_ref, pw_ref, pb_ref, g_ref, w1_ref, w3_ref, w2_ref,
                    o_ref, hm_ref):
    m = mod_ref[0]
    v = v_ref[0]
    mu = jnp.mean(v, axis=-1, keepdims=True)
    vc = v - mu
    var = jnp.mean(vc * vc, axis=-1, keepdims=True)
    y = _silu(vc * lax.rsqrt(var + EPS) * lng_ref[...] + lnb_ref[...]).astype(BF16)
    h = h_ref[0] + m[2:3] * (_dot(y, pw_ref[...]) + pb_ref[...])
    a = _rms_mod(h, g_ref[...], m[3:4], m[4:5]).astype(BF16)
    _swiglu_hidden(a, w1_ref, w3_ref, hm_ref)
    o_ref[0] = h + m[5:6] * _dot(hm_ref[...], w2_ref[...])


def _pw2_ffn(v, h, mods, ln_g, ln_b, pw, j, pb, g, w1, w3, w2, ctx_tiles):
    nb, l, d = h.shape
    f = w1.shape[-1]
    tok, mod = _row_specs(nb, d, 0, ctx_tiles)
    return pl.pallas_call(
        _pw2_ffn_kernel,
        out_shape=jax.ShapeDtypeStruct((nb, l, d), F32),
        grid=(nb, l // TM),
        in_specs=[tok(d), tok(d), mod, _full((1, d)), _full((1, d)), _layer((d, d), j), _full((1, d)),
                  _full((1, d)), _layer((d, f), j), _layer((d, f), j), _layer((f, d), j)],
        out_specs=tok(d),
        scratch_shapes=[pltpu.VMEM((TM, f), BF16)],
        input_output_aliases={1: 0},
        compiler_params=_cp("parallel", "arbitrary"),
        name="conv_pw2_ffn",
    )(v, h, mods, ln_g.reshape(1, d), ln_b.reshape(1, d), pw, pb.reshape(1, d), g.reshape(1, d), w1, w3, w2)


def _win_kernel(x_ref, mod_ref, g_ref, w_ref, lb_ref, p_ref):
    m = mod_ref[0]
    a = _rms_mod(x_ref[0], g_ref[...], m[0:1], m[1:2]).astype(BF16)
    d = x_ref.shape[-1]
    cw = 512
    for c in range(0, 5 * d, cw):
        part, off = divmod(c, d)
        y = _dot(a, w_ref[:, c:c + cw])
        if part in (1, 2):
            lb = lb_ref[part - 1:part, off:off + cw]
            y = jnp.log(lb + (1.0 - lb) * _sigmoid(y))
        elif part == 3:
            y = _silu(y) * (HEAD_DIM ** -0.5)
        elif part == 4:
            y = _silu(y)
        p_ref[0, :, c:c + cw] = y


def _rec_in_proj(h, mods, g, w, j, lower, ctx_tiles):
    nb, l, d = h.shape
    n = w.shape[-1]
    tok, mod = _row_specs(nb, d, 0, ctx_tiles)
    return pl.pallas_call(
        _win_kernel,
        out_shape=jax.ShapeDtypeStruct((nb, l, n), F32),
        grid=(nb, l // TM),
        in_specs=[tok(d), mod, _full((1, d)), _layer((d, n), j), _full((2, d))],
        out_specs=tok(n),
        compiler_params=_cp("parallel", "arbitrary"),
        name="rec_in_proj",
    )(h, mods, g.reshape(1, d), w, lower)


N_CHUNKS = BLK // CHUNK
N_CHUNK_SUMS = 7
SUM_ROWS = BLK + N_CHUNK_SUMS * N_CHUNKS + 8


def _rec_constants():
    t = np.arange(BLK)
    mats, masks = [], []
    for direction in range(2):
        p = t if direction == 0 else BLK - 1 - t
        pt, ps = p[:, None], p[None, :]
        ct, cs = pt // CHUNK, ps // CHUNK
        same = ct == cs
        mid4 = N_CHUNKS // 2
        mid2 = (ct // 4) * 4 + 2
        chunk_level = [
            same,
            cs < ct,
            cs > ct,
            (ct >= mid4) & (cs >= mid4) & (cs < ct),
            (ct < mid4) & (cs > ct) & (cs < mid4),
            (ct >= mid2) & (cs >= mid2) & (cs < ct),
            (ct < mid2) & (cs > ct) & (cs < mid2),
        ]
        rows = [same & (ps <= pt)]
        for m in chunk_level:
            m = np.broadcast_to(m, (BLK, BLK))
            rows.append(m[::CHUNK])
        rows.append(np.ones((8, BLK), bool))
        mats.append(np.concatenate([r.astype(np.float32) for r in rows], axis=0))
        m = [
            same & (ps <= pt),
            (ct % 2 == 1) & (cs == ct - 1),
            (ct // 4 == cs // 4) & (ct % 4 >= 2) & (cs % 4 < 2),
            (ct >= mid4) & (cs < mid4),
        ]
        level = np.full((BLK, BLK), float(len(m)), np.float32)
        for i, x in enumerate(m):
            level = np.where(np.broadcast_to(x, (BLK, BLK)), np.float32(i), level)
        masks.append(level)
    return np.stack(mats), np.stack(masks)


def _chunk_scale(x, ec, i):
    base = i * N_CHUNKS
    parts = []
    for j in range(N_CHUNKS):
        e = jnp.broadcast_to(ec[base + j:base + j + 1], (CHUNK, x.shape[1]))
        parts.append(x[CHUNK * j:CHUNK * (j + 1)] * e)
    return jnp.concatenate(parts, axis=0)


def _gla_kernel(v_ref, lff_ref, lfb_ref, q_ref, a_ref, m_ref, of_ref, ob_ref, *, ctx_blocks, n_blocks):
    last = n_blocks - 1
    n_heads = v_ref.shape[-1] // HEAD_DIM
    chains = [(dr, hd) for dr in range(2) for hd in range(n_heads)]
    lf_refs = (lff_ref, lfb_ref)
    o_refs = (of_ref, ob_ref)
    tot_row = N_CHUNK_SUMS * N_CHUNKS

    def block_rows(n):
        rev = jnp.where(n < ctx_blocks, ctx_blocks - 1 - n, last + ctx_blocks - n)
        return pl.multiple_of(n * BLK, BLK), pl.multiple_of(rev * BLK, BLK)

    def decay_sums(row0):
        sums = []
        for dr in range(2):
            logf = lf_refs[dr][0, pl.ds(row0[dr], BLK), :]
            hi = logf.astype(BF16)
            lo = (logf - hi.astype(F32)).astype(BF16)
            sums.append((logf, _dot(a_ref[dr], jnp.concatenate([hi, lo], axis=0))))
        return sums

    def operands(row0, sums):
        ops = []
        for dr, hd in chains:
            cols = slice(hd * HEAD_DIM, (hd + 1) * HEAD_DIM)
            logf, s = sums[dr]
            s = s[:, cols]
            k = 1.0 - jnp.exp(logf[:, cols])
            qs = q_ref[0, pl.ds(row0[dr], BLK), cols]
            bcum = s[0:BLK]
            ec = jnp.exp(s[BLK:SUM_ROWS])
            q_in = qs * jnp.exp(bcum)
            k_in = k * jnp.exp(-bcum)
            k_st = _chunk_scale(k_in, ec, 0)
            ops.append(dict(
                q_blk=_chunk_scale(q_in, ec, 1).astype(BF16),
                k_end=_chunk_scale(k_st, ec, 2).astype(BF16),
                q4=_chunk_scale(q_in, ec, 3).astype(BF16),
                k4=_chunk_scale(k_st, ec, 4).astype(BF16),
                q2=_chunk_scale(q_in, ec, 5).astype(BF16),
                k2=_chunk_scale(k_st, ec, 6).astype(BF16),
                e_tot=ec[tot_row:tot_row + 1],
                q_in=q_in.astype(BF16), k_in=k_in.astype(BF16), k_st=k_st.astype(BF16),
                vb=v_ref[0, pl.ds(row0[dr], BLK), cols].astype(BF16)))
        return ops

    def products(ops, states):
        return [(_dot_nt(o["q_in"], o["k_in"]), _dot_nt(o["q_in"], o["k_st"]),
                 _dot_nt(o["q2"], o["k2"]), _dot_nt(o["q4"], o["k4"]),
                 _dot_nt(o["q_blk"], st.astype(BF16)), _dot_tn(o["vb"], o["k_end"]))
                for o, st in zip(ops, states)]

    def finish(row0, ops, raw, states):
        atts = []
        for (dr, hd), r in zip(chains, raw):
            level = m_ref[dr]
            att = jnp.where(level == 0.0, r[0], jnp.where(level == 1.0, r[1], jnp.where(
                level == 2.0, r[2], jnp.where(level == 3.0, r[3], 0.0))))
            atts.append(att.astype(BF16))
        new_states = []
        for (dr, hd), o, r, att, st in zip(chains, ops, raw, atts, states):
            cols = slice(hd * HEAD_DIM, (hd + 1) * HEAD_DIM)
            o_refs[dr][0, pl.ds(row0[dr], BLK), cols] = (_dot(att, o["vb"]) + r[4]).astype(o_refs[dr].dtype)
            new_states.append(o["e_tot"] * st + r[5])
        return tuple(new_states)

    def step(n, states):
        row0 = block_rows(n)
        ops = operands(row0, decay_sums(row0))
        return finish(row0, ops, products(ops, states), states)

    zero = jnp.zeros((HEAD_DIM, HEAD_DIM), F32)
    lax.fori_loop(0, n_blocks, step, (zero,) * len(chains))


GLA_HEADS_PER_STEP = 4


def _gla(p, ctx):
    nb, l, n5 = p.shape
    d = n5 // 5
    cw = GLA_HEADS_PER_STEP * HEAD_DIM
    groups = d // cw
    amat, masks = _rec_constants()
    amat = np.concatenate([amat, amat], axis=2)
    kern = functools.partial(_gla_kernel, ctx_blocks=ctx // BLK, n_blocks=l // BLK)
    col = lambda base: pl.BlockSpec((1, l, cw), lambda b, h: (b, 0, base + h))
    out = jax.ShapeDtypeStruct((nb, l, d), BF16)
    return pl.pallas_call(
        kern,
        out_shape=(out, out),
        grid=(nb, groups),
        in_specs=[col(0), col(groups), col(2 * groups), col(3 * groups),
                  _full(amat.shape), _full(masks.shape)],
        out_specs=(col(0), col(0)),
        compiler_params=_cp("parallel", "arbitrary"),
        name="hgrn2_gla",
    )(p, p, p, p, jnp.asarray(amat, BF16), jnp.asarray(masks, F32))


def _route_kernel(of_ref, ob_ref, gate_ref, h_ref, mod_ref, ng_ref, wo_ref, g_ref, r_ref, tri_ref,
                  hn_ref, a_ref, ri_ref, rw_ref, cnt_ref, carry_ref, y_ref, *, n_experts):
    @pl.when((pl.program_id(0) == 0) & (pl.program_id(1) == 0))
    def _():
        carry_ref[...] = jnp.zeros_like(carry_ref)

    m = mod_ref[0]
    d = of_ref.shape[-1]
    for c in range(0, d, HEAD_DIM):
        o = of_ref[0, :, c:c + HEAD_DIM].astype(F32) + ob_ref[0, :, c:c + HEAD_DIM].astype(F32)
        o = o * lax.rsqrt(jnp.mean(o * o, axis=-1, keepdims=True) + EPS) * ng_ref[:, c:c + HEAD_DIM]
        y_ref[:, c:c + HEAD_DIM] = (o * gate_ref[0, :, c:c + HEAD_DIM]).astype(BF16)
    h = h_ref[0] + m[2:3] * _dot(y_ref[...], wo_ref[...])
    hn_ref[0] = h
    a = _rms_mod(h, g_ref[...], m[3:4], m[4:5])
    a_ref[0] = a
    a_hi = a.astype(BF16)
    a_lo = (a - a_hi.astype(F32)).astype(BF16)
    part = _dot(a_hi, r_ref[...])
    logits = part[:, :LANES] + part[:, LANES:] + _dot(a_lo, r_ref[:, :LANES])
    lane = lax.broadcasted_iota(jnp.int32, logits.shape, 1)
    neg = jnp.float32(-jnp.inf)
    logits = jnp.where(lane < n_experts, logits, neg)
    v1 = jnp.max(logits, axis=-1, keepdims=True)
    i1 = jnp.min(jnp.where(logits == v1, lane, LANES), axis=-1, keepdims=True)
    rest = jnp.where(lane == i1, neg, logits)
    v2 = jnp.max(rest, axis=-1, keepdims=True)
    i2 = jnp.min(jnp.where(rest == v2, lane, LANES), axis=-1, keepdims=True)
    e2 = jnp.exp(v2 - v1)
    w1 = 1.0 / (1.0 + e2)
    w2 = e2 / (1.0 + e2)
    oh1 = jnp.where(lane == i1, 1.0, 0.0)
    oh2 = jnp.where(lane == i2, 1.0, 0.0)
    both = oh1 + oh2
    earlier = _dot(tri_ref[...], both.astype(BF16)) + carry_ref[...]
    r1 = jnp.sum(oh1 * earlier, axis=-1, keepdims=True).astype(jnp.int32)
    r2 = jnp.sum(oh2 * earlier, axis=-1, keepdims=True).astype(jnp.int32)
    carry = carry_ref[...] + jnp.sum(both, axis=0, keepdims=True)
    carry_ref[...] = carry
    cnt_ref[...] = jnp.broadcast_to(carry, cnt_ref.shape).astype(jnp.int32)
    ri_ref[0] = jnp.where(lane == 0, i1, jnp.where(lane == 1, i2, jnp.where(lane == 2, r1,
                          jnp.where(lane == 3, r2, 0))))
    rw_ref[0] = jnp.where(lane == 0, w1, jnp.where(lane == 1, w2, 0.0))


def _readout_route(o_f, o_b, p, h, mods, onorm_g, w_o, j, g, router, ctx_tiles, t0):
    nb, l, d = h.shape
    n_experts = router.shape[-1]
    nt = l // TM - t0
    tok, mod = _row_specs(nb, d, t0, ctx_tiles)
    gate_cols = pl.BlockSpec((1, TM, d), lambda b, t: (b, t + t0, 4))
    out_tok = lambda width: pl.BlockSpec((1, TM, width), lambda b, t: (b, t, 0))
    rpad = jnp.pad(router, ((0, 0), (0, LANES - n_experts)))
    r_hi = rpad.astype(BF16)
    r_split = jnp.concatenate([r_hi, (rpad - r_hi.astype(F32)).astype(BF16)], axis=1)
    tri = jnp.asarray(np.tril(np.ones((TM, TM), np.float32), -1), BF16)
    return pl.pallas_call(
        functools.partial(_route_kernel, n_experts=n_experts),
        out_shape=(jax.ShapeDtypeStruct((nb, l, d), F32),
                   jax.ShapeDtypeStruct((nb, nt * TM, d), F32),
                   jax.ShapeDtypeStruct((nb, nt * TM, LANES), jnp.int32),
                   jax.ShapeDtypeStruct((nb, nt * TM, LANES), F32),
                   jax.ShapeDtypeStruct((8, LANES), jnp.int32)),
        grid=(nb, nt),
        in_specs=[tok(d), tok(d), gate_cols, tok(d), mod, _full((1, d)), _layer((d, d), j),
                  _full((1, d)), _full((d, 2 * LANES)), _full((TM, TM))],
        out_specs=(tok(d), out_tok(d), out_tok(LANES), out_tok(LANES), _full((8, LANES))),
        scratch_shapes=[pltpu.VMEM((1, LANES), F32), pltpu.VMEM((TM, d), BF16)],
        input_output_aliases={3: 0},
        compiler_params=_cp("arbitrary", "arbitrary"),
        name="rec_readout_route",
    )(o_f, o_b, p, h, mods, onorm_g.reshape(1, d), w_o, g.reshape(1, d), r_split, tri)


def _row_copy(src, i, dst, j, sem):
    return pltpu.make_async_copy(src.at[pl.ds(i, 1)], dst.at[pl.ds(j, 1)], sem)


def _dispatch_kernel(pos_ref, pad_ref, nv_ref, a_ref, xs_hbm, zero_ref, sems, *, n_tok_tiles):
    step = pl.program_id(0)
    base = step * TM
    tile = a_ref.at[0]

    for r in range(TM):
        t = base + r
        _row_copy(tile, r, xs_hbm, pos_ref[2 * t], sems.at[0]).start(priority=0)
        _row_copy(tile, r, xs_hbm, pos_ref[2 * t + 1], sems.at[0]).start(priority=1)
    for _ in range(TOP_K):
        pltpu.make_async_copy(tile, xs_hbm.at[pl.ds(0, TM)], sems.at[0]).wait()

    @pl.when(step == n_tok_tiles - 1)
    def _fill():
        zero_ref[...] = jnp.zeros_like(zero_ref)

        def pad_start(i, carry):
            @pl.when(pad_ref[i] >= 0)
            def _():
                _row_copy(zero_ref, 0, xs_hbm, pad_ref[i], sems.at[1]).start()
            return carry

        def pad_wait(i, carry):
            @pl.when(pad_ref[i] >= 0)
            def _():
                _row_copy(zero_ref, 0, xs_hbm, 0, sems.at[1]).wait()
            return carry

        lax.fori_loop(0, pad_ref.shape[0], pad_start, 0)
        lax.fori_loop(0, pad_ref.shape[0], pad_wait, 0)

        n_tiles = xs_hbm.shape[0] // TM
        n_tail = n_tiles - TOP_K * n_tok_tiles

        def tail_copy(i):
            row0 = pl.multiple_of((nv_ref[0] + i) * TM, TM)
            return pltpu.make_async_copy(zero_ref, xs_hbm.at[pl.ds(row0, TM)], sems.at[1])

        def tail_start(i, carry):
            @pl.when(nv_ref[0] + i < n_tiles)
            def _():
                tail_copy(i).start()
            return carry

        def tail_wait(i, carry):
            @pl.when(nv_ref[0] + i < n_tiles)
            def _():
                tail_copy(i).wait()
            return carry

        lax.fori_loop(0, n_tail, tail_start, 0)
        lax.fori_loop(0, n_tail, tail_wait, 0)


def _moe_dispatch(a, pos, pad_rows, n_valid, n_rows):
    nb, n, d = a.shape
    nt = n // TM
    kern = functools.partial(_dispatch_kernel, n_tok_tiles=nb * nt)
    return pl.pallas_call(
        kern,
        out_shape=jax.ShapeDtypeStruct((n_rows, d), a.dtype),
        grid_spec=pltpu.PrefetchScalarGridSpec(
            num_scalar_prefetch=3, grid=(nb * nt,),
            in_specs=[pl.BlockSpec((1, TM, d), lambda i, p, q, v: (i // nt, i % nt, 0))],
            out_specs=pl.BlockSpec(memory_space=pl.ANY),
            scratch_shapes=[pltpu.VMEM((TM, d), a.dtype), pltpu.SemaphoreType.DMA((2,))]),
        compiler_params=_cp("arbitrary"),
        name="moe_dispatch",
    )(pos, pad_rows, n_valid, a)


def _experts_kernel(te_ref, xi_ref, nv_ref, x_ref, w1_ref, w3_ref, w2_ref, y_ref, hm_ref):
    @pl.when(pl.program_id(0) < nv_ref[0])
    def _():
        _swiglu_hidden(x_ref[...].astype(BF16), w1_ref.at[0], w3_ref.at[0], hm_ref)
        y_ref[...] = _dot(hm_ref[...], w2_ref[0])

    @pl.when(pl.program_id(0) >= nv_ref[0])
    def _():
        y_ref[...] = jnp.zeros_like(y_ref)


def _moe_experts(xs, tile_expert, x_index, n_valid, w1, w3, w2, j, n_tiles):
    dp = xs.shape[-1]
    d, f = w1.shape[-2:]
    rows = pl.BlockSpec((TM, dp), lambda i, te, xi, nv: (xi[i], 0))
    wspec = lambda s: pl.BlockSpec((None, 1) + s, lambda i, te, xi, nv: (j, te[i], 0, 0))
    return pl.pallas_call(
        _experts_kernel,
        out_shape=jax.ShapeDtypeStruct((n_tiles * TM, dp), xs.dtype),
        grid_spec=pltpu.PrefetchScalarGridSpec(
            num_scalar_prefetch=3, grid=(n_tiles,),
            in_specs=[rows, wspec((d, f)), wspec((d, f)), wspec((f, d))],
            out_specs=pl.BlockSpec((TM, dp), lambda i, te, xi, nv: (i, 0)),
            scratch_shapes=[pltpu.VMEM((TM, f), BF16)]),
        compiler_params=_cp("arbitrary"),
        name="moe_experts",
    )(tile_expert, x_index, n_valid, xs, w1, w3, w2)


def _combine_kernel(pos_ref, h_ref, mod_ref, rw_ref, ys_hbm, o_ref, y1_ref, y2_ref, sems, *, nt, fg_ref=None):
    base = (pl.program_id(0) * nt + pl.program_id(1)) * TM

    for r in range(TM):
        t = base + r
        _row_copy(ys_hbm, pos_ref[2 * t], y1_ref, r, sems.at[0]).start(priority=0)
        _row_copy(ys_hbm, pos_ref[2 * t + 1], y2_ref, r, sems.at[1]).start(priority=1)
    pltpu.make_async_copy(ys_hbm.at[pl.ds(0, TM)], y1_ref, sems.at[0]).wait()
    pltpu.make_async_copy(ys_hbm.at[pl.ds(0, TM)], y2_ref, sems.at[1]).wait()
    rw = rw_ref[0]
    y = rw[:, 0:1] * y1_ref[...] + rw[:, 1:2] * y2_ref[...]
    out = h_ref[0] + mod_ref[0][5:6] * y
    if fg_ref is not None:
        out = out * lax.rsqrt(jnp.mean(out * out, axis=-1, keepdims=True) + EPS) * fg_ref[...]
    o_ref[0] = out


def _combine_final_kernel(pos_ref, h_ref, mod_ref, rw_ref, fg_ref, ys_hbm, o_ref, y1_ref, y2_ref, sems, *, nt):
    _combine_kernel(pos_ref, h_ref, mod_ref, rw_ref, ys_hbm, o_ref, y1_ref, y2_ref, sems, nt=nt, fg_ref=fg_ref)


def _moe_combine(ys, pos, rw, h, mods, ctx_tiles, t0, final_g=None):
    nb, l, d = h.shape
    nt = l // TM - t0
    tok = lambda width, off: pl.BlockSpec((1, TM, width), lambda b, t, p: (b, t + off, 0))
    mod = pl.BlockSpec((1, 6, d), lambda b, t, p: (jnp.where(t + t0 < ctx_tiles, nb, b), 0, 0))
    scratch = [pltpu.VMEM((TM, ys.shape[-1]), ys.dtype), pltpu.VMEM((TM, ys.shape[-1]), ys.dtype),
               pltpu.SemaphoreType.DMA((2,))]
    any_spec = pl.BlockSpec(memory_space=pl.ANY)
    if final_g is None:
        return pl.pallas_call(
            functools.partial(_combine_kernel, nt=nt),
            out_shape=jax.ShapeDtypeStruct((nb, l, d), F32),
            grid_spec=pltpu.PrefetchScalarGridSpec(
                num_scalar_prefetch=1, grid=(nb, nt),
                in_specs=[tok(d, t0), mod, tok(LANES, 0), any_spec],
                out_specs=tok(d, t0), scratch_shapes=scratch),
            input_output_aliases={1: 0},
            compiler_params=_cp("arbitrary", "arbitrary"),
            name="moe_combine",
        )(pos, h, mods, rw, ys)
    return pl.pallas_call(
        functools.partial(_combine_final_kernel, nt=nt),
        out_shape=jax.ShapeDtypeStruct((nb, nt * TM, d), F32),
        grid_spec=pltpu.PrefetchScalarGridSpec(
            num_scalar_prefetch=1, grid=(nb, nt),
            in_specs=[tok(d, t0), mod, tok(LANES, 0), pl.BlockSpec((1, d), lambda b, t, p: (0, 0)), any_spec],
            out_specs=tok(d, 0), scratch_shapes=scratch),
        compiler_params=_cp("arbitrary", "arbitrary"),
        name="moe_combine_final",
    )(pos, h, mods, rw, final_g.reshape(1, d), ys)


def _moe(h, a, ri, rw, counts, mods, n_experts, w1, w3, w2, j, ctx_tiles, t0, final_g=None):
    nb, n, d = a.shape
    tokens = nb * n
    n_tiles = (TOP_K * tokens) // TM + n_experts
    counts = counts[0, :n_experts]
    padded = ((counts + TM - 1) // TM) * TM
    ends = jnp.cumsum(padded)
    starts = ends - padded
    ri = ri.reshape(tokens, LANES)
    pos = (starts[ri[:, 0:TOP_K]] + ri[:, TOP_K:2 * TOP_K]).reshape(TOP_K * tokens)
    n_valid = ends[-1] // TM
    x_index = jnp.maximum(jnp.minimum(jnp.arange(n_tiles, dtype=jnp.int32), n_valid - 1), 0)
    tile_expert = jnp.minimum(jnp.sum(ends[None, :] <= (x_index * TM)[:, None], axis=1), n_experts - 1)
    fill = jnp.arange(TM, dtype=jnp.int32)[None, :]
    pad_rows = jnp.where(fill < (padded - counts)[:, None], (starts + counts)[:, None] + fill, -1)
    n_valid = n_valid.reshape(1).astype(jnp.int32)
    xs = _moe_dispatch(a, pos.astype(jnp.int32),
                       pad_rows.reshape(-1).astype(jnp.int32), n_valid, n_tiles * TM)
    ys = _moe_experts(xs, tile_expert.astype(jnp.int32), x_index, n_valid, w1, w3, w2, j, n_tiles)
    return _moe_combine(ys, pos.astype(jnp.int32), rw, h, mods, ctx_tiles, t0, final_g)


def kernel(x, c, ctx, c_ctx, ada_w, ada_b, norm_mix_g, norm_ffn_g, final_g, conv_pw1_w, conv_pw1_b, conv_dw_w, conv_dw_b, conv_ln_g, conv_ln_b, conv_pw2_w, conv_pw2_b, rec_w_in, rec_lb_logits, rec_onorm_g, rec_w_o, ffn_w1, ffn_w3, ffn_w2, moe_router, moe_w1, moe_w3, moe_w2):
    nb, seq, d = x.shape
    n_ctx = ctx.shape[1]
    depth = ada_w.shape[0]
    ctx_tiles = n_ctx // TM
    assert n_ctx % TM == 0 and seq % TM == 0 and n_ctx % BLK == 0 and seq % (GRID_W * 8) == 0
    assert d % (2 * 256) == 0 and nb < 16 and depth % 2 == 0

    cond = jnp.zeros((16, d), F32).at[:nb].set(c).at[nb].set(c_ctx)
    mods = _modulation(cond, ada_w, ada_b)
    sm = jax.nn.softmax(rec_lb_logits.astype(F32), axis=1)
    lower = jnp.cumsum(sm, axis=1) - sm[:, :1]

    bf = lambda w: w.astype(BF16)
    conv_pw1_w, conv_pw2_w, rec_w_in, rec_w_o = bf(conv_pw1_w), bf(conv_pw2_w), bf(rec_w_in), bf(rec_w_o)
    ffn_w1, ffn_w3, ffn_w2 = bf(ffn_w1), bf(ffn_w3), bf(ffn_w2)
    moe_w1, moe_w3, moe_w2 = bf(moe_w1), bf(moe_w3), bf(moe_w2)

    h = jnp.concatenate([ctx, x], axis=1)
    for i in range(depth):
        last = i == depth - 1
        j = i // 2
        t0 = ctx_tiles if last else 0
        if i % 2 == 0:
            u = _pw1_glu(h, mods[i], norm_mix_g[i], conv_pw1_w, j, conv_pw1_b[j], ctx_tiles)
            v = _dwconv(u, conv_dw_w[j], conv_dw_b[j], n_ctx)
            h = _pw2_ffn(v, h, mods[i], conv_ln_g[j], conv_ln_b[j], conv_pw2_w, j, conv_pw2_b[j],
                         norm_ffn_g[i], ffn_w1, ffn_w3, ffn_w2, ctx_tiles)
        else:
            p = _rec_in_proj(h, mods[i], norm_mix_g[i], rec_w_in, j, lower[:, i], ctx_tiles)
            o_f, o_b = _gla(p, n_ctx)
            h, a, ri, rw, counts = _readout_route(o_f, o_b, p, h, mods[i], rec_onorm_g[j], rec_w_o, j,
                                                  norm_ffn_g[i], moe_router[j], ctx_tiles, t0)
            h = _moe(h, a, ri, rw, counts, mods[i], moe_router.shape[-1], moe_w1, moe_w3, moe_w2, j,
                     ctx_tiles, t0, final_g if last else None)
    return h
```

```python
import functools

import numpy as np
import jax
import jax.numpy as jnp
from jax import lax
from jax.experimental import pallas as pl
from jax.experimental.pallas import tpu as pltpu

EPS = 1e-6
GRID_W = 64
CONV_PAD = 16
HEAD_DIM = 128
BLK = 128
CHUNK = 16
TOP_K = 2
TM = 256
LANES = 128
VMEM_LIMIT = 56 * 1024 * 1024

F32 = jnp.float32
BF16 = jnp.bfloat16


def _cp(*sem):
    return pltpu.CompilerParams(dimension_semantics=sem, vmem_limit_bytes=VMEM_LIMIT)


def _dot(a, b):
    return jnp.dot(a, b, preferred_element_type=F32)


def _dot_nt(a, b):
    return lax.dot_general(a, b, (((1,), (1,)), ((), ())), preferred_element_type=F32)


def _dot_tn(a, b):
    return lax.dot_general(a, b, (((0,), (0,)), ((), ())), preferred_element_type=F32)


def _sigmoid(x):
    return 1.0 / (1.0 + jnp.exp(-x))


def _silu(x):
    return x * _sigmoid(x)


def _rms_mod(x, g, shift, scale):
    ms = jnp.mean(x * x, axis=-1, keepdims=True)
    return (x * lax.rsqrt(ms + EPS) * g) * (1.0 + scale) + shift


def _mod_kernel(cond_ref, w_ref, b_ref, o_ref):
    s = _silu(cond_ref[...])
    o_ref[0] = jnp.dot(s, w_ref[0], preferred_element_type=F32,
                       precision=lax.Precision.HIGHEST) + b_ref[0]


def _modulation(cond, ada_w, ada_b):
    depth, d, n = ada_w.shape
    tn = n // 4
    out = pl.pallas_call(
        _mod_kernel,
        out_shape=jax.ShapeDtypeStruct((depth, cond.shape[0], n), F32),
        grid=(depth, n // tn),
        in_specs=[
            pl.BlockSpec(cond.shape, lambda l, j: (0, 0)),
            pl.BlockSpec((1, d, tn), lambda l, j: (l, 0, j)),
            pl.BlockSpec((1, 1, tn), lambda l, j: (l, 0, j)),
        ],
        out_specs=pl.BlockSpec((1, cond.shape[0], tn), lambda l, j: (l, 0, j)),
        compiler_params=_cp("arbitrary", "arbitrary"),
        name="modulation",
    )(cond, ada_w, ada_b.reshape(depth, 1, n))
    return out.reshape(depth, cond.shape[0], 6, d)


def _row_specs(nb, d, t0, ctx_tiles):
    tok = lambda width: pl.BlockSpec((1, TM, width), lambda b, t: (b, t + t0, 0))
    mod = pl.BlockSpec((1, 6, d), lambda b, t: (jnp.where(t + t0 < ctx_tiles, nb, b), 0, 0))
    return tok, mod


def _full(shape):
    return pl.BlockSpec(shape, lambda b, t: (0,) * len(shape))


def _layer(shape, j):
    return pl.BlockSpec((None,) + shape, lambda b, t: (j,) + (0,) * len(shape),
                        pipeline_mode=pl.Buffered(1))


def _pw1_kernel(x_ref, mod_ref, g_ref, w_ref, b_ref, u_ref):
    m = mod_ref[0]
    a = _rms_mod(x_ref[0], g_ref[...], m[0:1], m[1:2]).astype(BF16)
    d = u_ref.shape[-1]
    cw = 512
    for c in range(0, d, cw):
        a1 = _dot(a, w_ref[:, c:c + cw]) + b_ref[:, c:c + cw]
        a2 = _dot(a, w_ref[:, d + c:d + c + cw]) + b_ref[:, d + c:d + c + cw]
        u_ref[0, :, c:c + cw] = a1 * _sigmoid(a2)


def _pw1_glu(h, mods, g, w, j, b, ctx_tiles):
    nb, l, d = h.shape
    tok, mod = _row_specs(nb, d, 0, ctx_tiles)
    return pl.pallas_call(
        _pw1_kernel,
        out_shape=jax.ShapeDtypeStruct((nb, l, d), F32),
        grid=(nb, l // TM),
        in_specs=[tok(d), mod, _full((1, d)), _layer((d, 2 * d), j), _full((1, 2 * d))],
        out_specs=tok(d),
        compiler_params=_cp("parallel", "arbitrary"),
        name="conv_pw1_glu",
    )(h, mods, g.reshape(1, d), w, b.reshape(1, 2 * d))


def _conv_kernel(u_ref, w_ref, b_ref, v_ref, pad_ref, *, ctx, rows, width_tiles):
    kw = w_ref.shape[0]
    half = (kw - 1) // 2
    cw = u_ref.shape[-1]
    j = pl.program_id(1)
    bias = jnp.broadcast_to(b_ref[...], (GRID_W, cw))
    zeros = jnp.zeros((CONV_PAD, cw), F32)

    def line_taps(win):
        acc = bias
        n = win.shape[0]
        shifted = {0: win}
        for k in range(kw):
            s = CONV_PAD + k - half
            b, a8 = s % 8, (s // 8) * 8
            if b not in shifted:
                shifted[b] = pltpu.roll(win, n - b, axis=0)
            acc = acc + w_ref[k:k + 1, :] * shifted[b][a8:a8 + GRID_W]
        return acc

    pad_ref[0:CONV_PAD] = zeros
    pad_ref[CONV_PAD:CONV_PAD + ctx] = u_ref[0, 0:ctx]
    pad_ref[CONV_PAD + ctx:2 * CONV_PAD + ctx] = zeros
    for c0 in range(0, ctx, GRID_W):
        v_ref[0, c0:c0 + GRID_W] = line_taps(pad_ref[c0:c0 + GRID_W + 2 * CONV_PAD])

    @pl.when(j < width_tiles)
    def _along_width():
        stride = GRID_W + CONV_PAD
        pad_ref[0:CONV_PAD] = zeros
        for r in range(rows):
            base = CONV_PAD + r * stride
            pad_ref[base:base + GRID_W] = u_ref[0, ctx + r * GRID_W:ctx + (r + 1) * GRID_W]
            pad_ref[base + GRID_W:base + stride] = zeros

        def body(r, carry):
            base = pl.multiple_of(r * stride, 8)
            win = pad_ref[pl.ds(base, GRID_W + 2 * CONV_PAD)]
            out0 = pl.multiple_of(ctx + r * GRID_W, 8)
            v_ref[0, pl.ds(out0, GRID_W)] = line_taps(win)
            return carry

        lax.fori_loop(0, rows, body, 0)

    @pl.when(j >= width_tiles)
    def _along_height():
        npad = half * GRID_W
        zrow = jnp.zeros((GRID_W, cw), F32)
        for r in range(half):
            pad_ref[r * GRID_W:(r + 1) * GRID_W] = zrow
            lo = npad + (rows + r) * GRID_W
            pad_ref[lo:lo + GRID_W] = zrow
        pad_ref[npad:npad + rows * GRID_W] = u_ref[0, ctx:ctx + rows * GRID_W]

        def body(r, carry):
            acc = bias
            for k in range(kw):
                src = pl.multiple_of((r + k) * GRID_W, 8)
                acc = acc + w_ref[k:k + 1, :] * pad_ref[pl.ds(src, GRID_W)]
            out0 = pl.multiple_of(ctx + r * GRID_W, 8)
            v_ref[0, pl.ds(out0, GRID_W)] = acc
            return carry

        lax.fori_loop(0, rows, body, 0)


def _dwconv(u, w, b, ctx):
    nb, l, d = u.shape
    kw = w.shape[0]
    cw = 256
    rows = (l - ctx) // GRID_W
    half = (kw - 1) // 2
    pad_rows = max(2 * CONV_PAD + ctx, CONV_PAD + rows * (GRID_W + CONV_PAD),
                   (rows + 2 * half) * GRID_W)
    kern = functools.partial(_conv_kernel, ctx=ctx, rows=rows, width_tiles=(d // 2) // cw)
    return pl.pallas_call(
        kern,
        out_shape=jax.ShapeDtypeStruct((nb, l, d), F32),
        grid=(nb, d // cw),
        in_specs=[
            pl.BlockSpec((1, l, cw), lambda b_, j: (b_, 0, j)),
            pl.BlockSpec((kw, cw), lambda b_, j: (0, j)),
            pl.BlockSpec((1, cw), lambda b_, j: (0, j)),
        ],
        out_specs=pl.BlockSpec((1, l, cw), lambda b_, j: (b_, 0, j)),
        scratch_shapes=[pltpu.VMEM((pad_rows, cw), F32)],
        compiler_params=_cp("parallel", "arbitrary"),
        name="axial_dwconv",
    )(u, w, b.reshape(1, d))


def _swiglu_hidden(a, w1_ref, w3_ref, hm_ref):
    f = hm_ref.shape[-1]
    cw = 256
    for c in range(0, f, cw):
        h1 = _dot(a, w1_ref[:, c:c + cw])
        h3 = _dot(a, w3_ref[:, c:c + cw])
        hm_ref[:, c:c + cw] = (_silu(h1) * h3).astype(BF16)


def _pw2_ffn_kernel(v_ref, h_ref, mod_ref, lng_ref, lnb_ref, pw_ref, pb_ref, g_ref, w1_ref, w3_ref, w2_ref,
                    e1_ref, e3_ref, e2_ref, o_ref, c1_ref, c3_ref, c2_ref, hm_ref):
    c1_ref[...] = e1_ref[...].astype(BF16)
    c3_ref[...] = e3_ref[...].astype(BF16)
    c2_ref[...] = e2_ref[...].astype(BF16)
    m = mod_ref[0]
    v = v_ref[0]
    mu = jnp.mean(v, axis=-1, keepdims=True)
    vc = v - mu
    var = jnp.mean(vc * vc, axis=-1, keepdims=True)
    y = _silu(vc * lax.rsqrt(var + EPS) * lng_ref[...] + lnb_ref[...]).astype(BF16)
    h = h_ref[0] + m[2:3] * (_dot(y, pw_ref[...]) + pb_ref[...])
    a = _rms_mod(h, g_ref[...], m[3:4], m[4:5]).astype(BF16)
    _swiglu_hidden(a, w1_ref, w3_ref, hm_ref)
    o_ref[0] = h + m[5:6] * _dot(hm_ref[...], w2_ref[...])


EXPERT_CAST_SLABS = 8


def _pw2_ffn(v, h, mods, ln_g, ln_b, pw, j, pb, g, w1, w3, w2, e1, e3, e2, ctx_tiles):
    nb, l, d = h.shape
    f = w1.shape[-1]
    nt = l // TM
    ne = e1.shape[1]
    slabs = ne * EXPERT_CAST_SLABS
    assert nb * nt >= slabs and d % (16 * EXPERT_CAST_SLABS) == 0 and f % (16 * EXPERT_CAST_SLABS) == 0
    tok, mod = _row_specs(nb, d, 0, ctx_tiles)

    def slab(rows, cols, layer):
        def index(b, t):
            s = jnp.minimum(b * nt + t, slabs - 1)
            lead = (s // EXPERT_CAST_SLABS, s % EXPERT_CAST_SLABS, 0)
            return ((j,) + lead) if layer else lead
        shape = (1, rows // EXPERT_CAST_SLABS, cols)
        return pl.BlockSpec(((None,) + shape) if layer else shape, index)

    cast = lambda rows, cols: jax.ShapeDtypeStruct((ne, rows, cols), BF16)
    return pl.pallas_call(
        _pw2_ffn_kernel,
        out_shape=(jax.ShapeDtypeStruct((nb, l, d), F32), cast(d, f), cast(d, f), cast(f, d)),
        grid=(nb, nt),
        in_specs=[tok(d), tok(d), mod, _full((1, d)), _full((1, d)), _layer((d, d), j), _full((1, d)),
                  _full((1, d)), _layer((d, f), j), _layer((d, f), j), _layer((f, d), j),
                  slab(d, f, True), slab(d, f, True), slab(f, d, True)],
        out_specs=(tok(d), slab(d, f, False), slab(d, f, False), slab(f, d, False)),
        scratch_shapes=[pltpu.VMEM((TM, f), BF16)],
        input_output_aliases={1: 0},
        compiler_params=_cp("arbitrary", "arbitrary"),
        name="conv_pw2_ffn",
    )(v, h, mods, ln_g.reshape(1, d), ln_b.reshape(1, d), pw, pb.reshape(1, d), g.reshape(1, d), w1, w3, w2,
      e1, e3, e2)


def _win_kernel(x_ref, mod_ref, g_ref, w_ref, lb_ref, p_ref):
    m = mod_ref[0]
    a = _rms_mod(x_ref[0], g_ref[...], m[0:1], m[1:2]).astype(BF16)
    d = x_ref.shape[-1]
    cw = 512
    for c in range(0, 5 * d, cw):
        part, off = divmod(c, d)
        y = _dot(a, w_ref[:, c:c + cw])
        if part in (1, 2):
            lb = lb_ref[part - 1:part, off:off + cw]
            y = jnp.log(lb + (1.0 - lb) * _sigmoid(y))
        elif part == 3:
            y = _silu(y) * (HEAD_DIM ** -0.5)
        elif part == 4:
            y = _silu(y)
        p_ref[0, :, c:c + cw] = y


def _rec_in_proj(h, mods, g, w, j, lower, ctx_tiles):
    nb, l, d = h.shape
    n = w.shape[-1]
    tok, mod = _row_specs(nb, d, 0, ctx_tiles)
    return pl.pallas_call(
        _win_kernel,
        out_shape=jax.ShapeDtypeStruct((nb, l, n), F32),
        grid=(nb, l // TM),
        in_specs=[tok(d), mod, _full((1, d)), _layer((d, n), j), _full((2, d))],
        out_specs=tok(n),
        compiler_params=_cp("parallel", "arbitrary"),
        name="rec_in_proj",
    )(h, mods, g.reshape(1, d), w, lower)


N_CHUNKS = BLK // CHUNK
N_CHUNK_SUMS = 7
SUM_ROWS = BLK + N_CHUNK_SUMS * N_CHUNKS + 8


def _rec_constants():
    t = np.arange(BLK)
    mats, masks = [], []
    for direction in range(2):
        p = t if direction == 0 else BLK - 1 - t
        pt, ps = p[:, None], p[None, :]
        ct, cs = pt // CHUNK, ps // CHUNK
        same = ct == cs
        mid4 = N_CHUNKS // 2
        mid2 = (ct // 4) * 4 + 2
        chunk_level = [
            same,
            cs < ct,
            cs > ct,
            (ct >= mid4) & (cs >= mid4) & (cs < ct),
            (ct < mid4) & (cs > ct) & (cs < mid4),
            (ct >= mid2) & (cs >= mid2) & (cs < ct),
            (ct < mid2) & (cs > ct) & (cs < mid2),
        ]
        rows = [same & (ps <= pt)]
        for m in chunk_level:
            m = np.broadcast_to(m, (BLK, BLK))
            rows.append(m[::CHUNK])
        rows.append(np.ones((8, BLK), bool))
        mats.append(np.concatenate([r.astype(np.float32) for r in rows], axis=0))
        m = [
            same & (ps <= pt),
            (ct % 2 == 1) & (cs == ct - 1),
            (ct // 4 == cs // 4) & (ct % 4 >= 2) & (cs % 4 < 2),
            (ct >= mid4) & (cs < mid4),
        ]
        level = np.full((BLK, BLK), float(len(m)), np.float32)
        for i, x in enumerate(m):
            level = np.where(np.broadcast_to(x, (BLK, BLK)), np.float32(i), level)
        masks.append(level)
    return np.stack(mats), np.stack(masks)


def _chunk_scale(x, ec, i):
    base = i * N_CHUNKS
    parts = []
    for j in range(N_CHUNKS):
        e = jnp.broadcast_to(ec[base + j:base + j + 1], (CHUNK, x.shape[1]))
        parts.append(x[CHUNK * j:CHUNK * (j + 1)] * e)
    return jnp.concatenate(parts, axis=0)


def _gla_kernel(v_ref, lff_ref, lfb_ref, q_ref, a_ref, m_ref, of_ref, ob_ref, *, ctx_blocks, n_blocks):
    last = n_blocks - 1
    n_heads = v_ref.shape[-1] // HEAD_DIM
    chains = [(dr, hd) for dr in range(2) for hd in range(n_heads)]
    lf_refs = (lff_ref, lfb_ref)
    o_refs = (of_ref, ob_ref)
    tot_row = N_CHUNK_SUMS * N_CHUNKS

    def block_rows(n):
        rev = jnp.where(n < ctx_blocks, ctx_blocks - 1 - n, last + ctx_blocks - n)
        return pl.multiple_of(n * BLK, BLK), pl.multiple_of(rev * BLK, BLK)

    def decay_sums(row0):
        sums = []
        for dr in range(2):
            logf = lf_refs[dr][0, pl.ds(row0[dr], BLK), :]
            hi = logf.astype(BF16)
            lo = (logf - hi.astype(F32)).astype(BF16)
            sums.append((logf, _dot(a_ref[dr], jnp.concatenate([hi, lo], axis=0))))
        return sums

    def operands(row0, sums):
        ops = []
        for dr, hd in chains:
            cols = slice(hd * HEAD_DIM, (hd + 1) * HEAD_DIM)
            logf, s = sums[dr]
            s = s[:, cols]
            k = 1.0 - jnp.exp(logf[:, cols])
            qs = q_ref[0, pl.ds(row0[dr], BLK), cols]
            bcum = s[0:BLK]
            ec = jnp.exp(s[BLK:SUM_ROWS])
            q_in = qs * jnp.exp(bcum)
            k_in = k * jnp.exp(-bcum)
            k_st = _chunk_scale(k_in, ec, 0)
            ops.append(dict(
                q_blk=_chunk_scale(q_in, ec, 1).astype(BF16),
                k_end=_chunk_scale(k_st, ec, 2).astype(BF16),
                q4=_chunk_scale(q_in, ec, 3).astype(BF16),
                k4=_chunk_scale(k_st, ec, 4).astype(BF16),
                q2=_chunk_scale(q_in, ec, 5).astype(BF16),
                k2=_chunk_scale(k_st, ec, 6).astype(BF16),
                e_tot=ec[tot_row:tot_row + 1],
                q_in=q_in.astype(BF16), k_in=k_in.astype(BF16), k_st=k_st.astype(BF16),
                vb=v_ref[0, pl.ds(row0[dr], BLK), cols].astype(BF16)))
        return ops

    def products(ops, states):
        return [(_dot_nt(o["q_in"], o["k_in"]), _dot_nt(o["q_in"], o["k_st"]),
                 _dot_nt(o["q2"], o["k2"]), _dot_nt(o["q4"], o["k4"]),
                 _dot_nt(o["q_blk"], st.astype(BF16)), _dot_tn(o["vb"], o["k_end"]))
                for o, st in zip(ops, states)]

    def finish(row0, ops, raw, states):
        atts = []
        for (dr, hd), r in zip(chains, raw):
            level = m_ref[dr]
            att = jnp.where(level == 0.0, r[0], jnp.where(level == 1.0, r[1], jnp.where(
                level == 2.0, r[2], jnp.where(level == 3.0, r[3], 0.0))))
            atts.append(att.astype(BF16))
        new_states = []
        for (dr, hd), o, r, att, st in zip(chains, ops, raw, atts, states):
            cols = slice(hd * HEAD_DIM, (hd + 1) * HEAD_DIM)
            o_refs[dr][0, pl.ds(row0[dr], BLK), cols] = (_dot(att, o["vb"]) + r[4]).astype(o_refs[dr].dtype)
            new_states.append(o["e_tot"] * st + r[5])
        return tuple(new_states)

    def step(n, states):
        row0 = block_rows(n)
        ops = operands(row0, decay_sums(row0))
        return finish(row0, ops, products(ops, states), states)

    zero = jnp.zeros((HEAD_DIM, HEAD_DIM), F32)
    lax.fori_loop(0, n_blocks, step, (zero,) * len(chains))


GLA_HEADS_PER_STEP = 4


def _gla(p, ctx):
    nb, l, n5 = p.shape
    d = n5 // 5
    cw = GLA_HEADS_PER_STEP * HEAD_DIM
    groups = d // cw
    amat, masks = _rec_constants()
    amat = np.concatenate([amat, amat], axis=2)
    kern = functools.partial(_gla_kernel, ctx_blocks=ctx // BLK, n_blocks=l // BLK)
    col = lambda base: pl.BlockSpec((1, l, cw), lambda b, h: (b, 0, base + h))
    out = jax.ShapeDtypeStruct((nb, l, d), BF16)
    return pl.pallas_call(
        kern,
        out_shape=(out, out),
        grid=(nb, groups),
        in_specs=[col(0), col(groups), col(2 * groups), col(3 * groups),
                  _full(amat.shape), _full(masks.shape)],
        out_specs=(col(0), col(0)),
        compiler_params=_cp("parallel", "arbitrary"),
        name="hgrn2_gla",
    )(p, p, p, p, jnp.asarray(amat, BF16), jnp.asarray(masks, F32))


def _route_kernel(of_ref, ob_ref, gate_ref, h_ref, mod_ref, ng_ref, wo_ref, g_ref, r_ref, tri_ref,
                  hn_ref, a_ref, ri_ref, rw_ref, cnt_ref, carry_ref, y_ref, *, n_experts):
    @pl.when((pl.program_id(0) == 0) & (pl.program_id(1) == 0))
    def _():
        carry_ref[...] = jnp.zeros_like(carry_ref)

    m = mod_ref[0]
    d = of_ref.shape[-1]
    for c in range(0, d, HEAD_DIM):
        o = of_ref[0, :, c:c + HEAD_DIM].astype(F32) + ob_ref[0, :, c:c + HEAD_DIM].astype(F32)
        o = o * lax.rsqrt(jnp.mean(o * o, axis=-1, keepdims=True) + EPS) * ng_ref[:, c:c + HEAD_DIM]
        y_ref[:, c:c + HEAD_DIM] = (o * gate_ref[0, :, c:c + HEAD_DIM]).astype(BF16)
    h = h_ref[0] + m[2:3] * _dot(y_ref[...], wo_ref[...])
    hn_ref[0] = h
    a = _rms_mod(h, g_ref[...], m[3:4], m[4:5])
    a_ref[0] = a
    a_hi = a.astype(BF16)
    a_lo = (a - a_hi.astype(F32)).astype(BF16)
    part = _dot(a_hi, r_ref[...])
    logits = part[:, :LANES] + part[:, LANES:] + _dot(a_lo, r_ref[:, :LANES])
    lane = lax.broadcasted_iota(jnp.int32, logits.shape, 1)
    neg = jnp.float32(-jnp.inf)
    logits = jnp.where(lane < n_experts, logits, neg)
    v1 = jnp.max(logits, axis=-1, keepdims=True)
    i1 = jnp.min(jnp.where(logits == v1, lane, LANES), axis=-1, keepdims=True)
    rest = jnp.where(lane == i1, neg, logits)
    v2 = jnp.max(rest, axis=-1, keepdims=True)
    i2 = jnp.min(jnp.where(rest == v2, lane, LANES), axis=-1, keepdims=True)
    e2 = jnp.exp(v2 - v1)
    w1 = 1.0 / (1.0 + e2)
    w2 = e2 / (1.0 + e2)
    oh1 = jnp.where(lane == i1, 1.0, 0.0)
    oh2 = jnp.where(lane == i2, 1.0, 0.0)
    both = oh1 + oh2
    earlier = _dot(tri_ref[...], both.astype(BF16)) + carry_ref[...]
    r1 = jnp.sum(oh1 * earlier, axis=-1, keepdims=True).astype(jnp.int32)
    r2 = jnp.sum(oh2 * earlier, axis=-1, keepdims=True).astype(jnp.int32)
    carry = carry_ref[...] + jnp.sum(both, axis=0, keepdims=True)
    carry_ref[...] = carry
    cnt_ref[...] = jnp.broadcast_to(carry, cnt_ref.shape).astype(jnp.int32)
    ri_ref[0] = jnp.where(lane == 0, i1, jnp.where(lane == 1, i2, jnp.where(lane == 2, r1,
                          jnp.where(lane == 3, r2, 0))))
    rw_ref[0] = jnp.where(lane == 0, w1, jnp.where(lane == 1, w2, 0.0))


def _readout_route(o_f, o_b, p, h, mods, onorm_g, w_o, j, g, router, ctx_tiles, t0):
    nb, l, d = h.shape
    n_experts = router.shape[-1]
    nt = l // TM - t0
    tok, mod = _row_specs(nb, d, t0, ctx_tiles)
    gate_cols = pl.BlockSpec((1, TM, d), lambda b, t: (b, t + t0, 4))
    out_tok = lambda width: pl.BlockSpec((1, TM, width), lambda b, t: (b, t, 0))
    rpad = jnp.pad(router, ((0, 0), (0, LANES - n_experts)))
    r_hi = rpad.astype(BF16)
    r_split = jnp.concatenate([r_hi, (rpad - r_hi.astype(F32)).astype(BF16)], axis=1)
    tri = jnp.asarray(np.tril(np.ones((TM, TM), np.float32), -1), BF16)
    return pl.pallas_call(
        functools.partial(_route_kernel, n_experts=n_experts),
        out_shape=(jax.ShapeDtypeStruct((nb, l, d), F32),
                   jax.ShapeDtypeStruct((nb, nt * TM, d), F32),
                   jax.ShapeDtypeStruct((nb, nt * TM, LANES), jnp.int32),
                   jax.ShapeDtypeStruct((nb, nt * TM, LANES), F32),
                   jax.ShapeDtypeStruct((8, LANES), jnp.int32)),
        grid=(nb, nt),
        in_specs=[tok(d), tok(d), gate_cols, tok(d), mod, _full((1, d)), _layer((d, d), j),
                  _full((1, d)), _full((d, 2 * LANES)), _full((TM, TM))],
        out_specs=(tok(d), out_tok(d), out_tok(LANES), out_tok(LANES), _full((8, LANES))),
        scratch_shapes=[pltpu.VMEM((1, LANES), F32), pltpu.VMEM((TM, d), BF16)],
        input_output_aliases={3: 0},
        compiler_params=_cp("arbitrary", "arbitrary"),
        name="rec_readout_route",
    )(o_f, o_b, p, h, mods, onorm_g.reshape(1, d), w_o, g.reshape(1, d), r_split, tri)


def _row_copy(src, i, dst, j, sem):
    return pltpu.make_async_copy(src.at[pl.ds(i, 1)], dst.at[pl.ds(j, 1)], sem)


def _dispatch_kernel(pos_ref, pad_ref, nv_ref, a_ref, xs_hbm, zero_ref, sems, *, n_tok_tiles):
    step = pl.program_id(0)
    base = step * TM
    tile = a_ref.at[0]

    for r in range(TM):
        t = base + r
        _row_copy(tile, r, xs_hbm, pos_ref[2 * t], sems.at[0]).start(priority=0)
        _row_copy(tile, r, xs_hbm, pos_ref[2 * t + 1], sems.at[0]).start(priority=1)
    for _ in range(TOP_K):
        pltpu.make_async_copy(tile, xs_hbm.at[pl.ds(0, TM)], sems.at[0]).wait()

    @pl.when(step == n_tok_tiles - 1)
    def _fill():
        zero_ref[...] = jnp.zeros_like(zero_ref)

        def pad_start(i, carry):
            @pl.when(pad_ref[i] >= 0)
            def _():
                _row_copy(zero_ref, 0, xs_hbm, pad_ref[i], sems.at[1]).start()
            return carry

        def pad_wait(i, carry):
            @pl.when(pad_ref[i] >= 0)
            def _():
                _row_copy(zero_ref, 0, xs_hbm, 0, sems.at[1]).wait()
            return carry

        lax.fori_loop(0, pad_ref.shape[0], pad_start, 0)
        lax.fori_loop(0, pad_ref.shape[0], pad_wait, 0)

        n_tiles = xs_hbm.shape[0] // TM
        n_tail = n_tiles - TOP_K * n_tok_tiles

        def tail_copy(i):
            row0 = pl.multiple_of((nv_ref[0] + i) * TM, TM)
            return pltpu.make_async_copy(zero_ref, xs_hbm.at[pl.ds(row0, TM)], sems.at[1])

        def tail_start(i, carry):
            @pl.when(nv_ref[0] + i < n_tiles)
            def _():
                tail_copy(i).start()
            return carry

        def tail_wait(i, carry):
            @pl.when(nv_ref[0] + i < n_tiles)
            def _():
                tail_copy(i).wait()
            return carry

        lax.fori_loop(0, n_tail, tail_start, 0)
        lax.fori_loop(0, n_tail, tail_wait, 0)


def _moe_dispatch(a, pos, pad_rows, n_valid, n_rows):
    nb, n, d = a.shape
    nt = n // TM
    kern = functools.partial(_dispatch_kernel, n_tok_tiles=nb * nt)
    return pl.pallas_call(
        kern,
        out_shape=jax.ShapeDtypeStruct((n_rows, d), a.dtype),
        grid_spec=pltpu.PrefetchScalarGridSpec(
            num_scalar_prefetch=3, grid=(nb * nt,),
            in_specs=[pl.BlockSpec((1, TM, d), lambda i, p, q, v: (i // nt, i % nt, 0))],
            out_specs=pl.BlockSpec(memory_space=pl.ANY),
            scratch_shapes=[pltpu.VMEM((TM, d), a.dtype), pltpu.SemaphoreType.DMA((2,))]),
        compiler_params=_cp("arbitrary"),
        name="moe_dispatch",
    )(pos, pad_rows, n_valid, a)


def _experts_kernel(te_ref, xi_ref, nv_ref, x_ref, w1_ref, w3_ref, w2_ref, y_ref, hm_ref):
    @pl.when(pl.program_id(0) < nv_ref[0])
    def _():
        _swiglu_hidden(x_ref[...].astype(BF16), w1_ref.at[0], w3_ref.at[0], hm_ref)
        y_ref[...] = _dot(hm_ref[...], w2_ref[0])

    @pl.when(pl.program_id(0) >= nv_ref[0])
    def _():
        y_ref[...] = jnp.zeros_like(y_ref)


def _moe_experts(xs, tile_expert, x_index, n_valid, w1, w3, w2, n_tiles):
    dp = xs.shape[-1]
    d, f = w1.shape[-2:]
    rows = pl.BlockSpec((TM, dp), lambda i, te, xi, nv: (xi[i], 0))
    wspec = lambda s: pl.BlockSpec((1,) + s, lambda i, te, xi, nv: (te[i], 0, 0))
    return pl.pallas_call(
        _experts_kernel,
        out_shape=jax.ShapeDtypeStruct((n_tiles * TM, dp), xs.dtype),
        grid_spec=pltpu.PrefetchScalarGridSpec(
            num_scalar_prefetch=3, grid=(n_tiles,),
            in_specs=[rows, wspec((d, f)), wspec((d, f)), wspec((f, d))],
            out_specs=pl.BlockSpec((TM, dp), lambda i, te, xi, nv: (i, 0)),
            scratch_shapes=[pltpu.VMEM((TM, f), BF16)]),
        compiler_params=_cp("arbitrary"),
        name="moe_experts",
    )(tile_expert, x_index, n_valid, xs, w1, w3, w2)


def _combine_kernel(pos_ref, h_ref, mod_ref, rw_ref, ys_hbm, o_ref, y1_ref, y2_ref, sems, *, nt, fg_ref=None):
    base = (pl.program_id(0) * nt + pl.program_id(1)) * TM

    for r in range(TM):
        t = base + r
        _row_copy(ys_hbm, pos_ref[2 * t], y1_ref, r, sems.at[0]).start(priority=0)
        _row_copy(ys_hbm, pos_ref[2 * t + 1], y2_ref, r, sems.at[1]).start(priority=1)
    pltpu.make_async_copy(ys_hbm.at[pl.ds(0, TM)], y1_ref, sems.at[0]).wait()
    pltpu.make_async_copy(ys_hbm.at[pl.ds(0, TM)], y2_ref, sems.at[1]).wait()
    rw = rw_ref[0]
    y = rw[:, 0:1] * y1_ref[...] + rw[:, 1:2] * y2_ref[...]
    out = h_ref[0] + mod_ref[0][5:6] * y
    if fg_ref is not None:
        out = out * lax.rsqrt(jnp.mean(out * out, axis=-1, keepdims=True) + EPS) * fg_ref[...]
    o_ref[0] = out


def _combine_final_kernel(pos_ref, h_ref, mod_ref, rw_ref, fg_ref, ys_hbm, o_ref, y1_ref, y2_ref, sems, *, nt):
    _combine_kernel(pos_ref, h_ref, mod_ref, rw_ref, ys_hbm, o_ref, y1_ref, y2_ref, sems, nt=nt, fg_ref=fg_ref)


def _moe_combine(ys, pos, rw, h, mods, ctx_tiles, t0, final_g=None):
    nb, l, d = h.shape
    nt = l // TM - t0
    tok = lambda width, off: pl.BlockSpec((1, TM, width), lambda b, t, p: (b, t + off, 0))
    mod = pl.BlockSpec((1, 6, d), lambda b, t, p: (jnp.where(t + t0 < ctx_tiles, nb, b), 0, 0))
    scratch = [pltpu.VMEM((TM, ys.shape[-1]), ys.dtype), pltpu.VMEM((TM, ys.shape[-1]), ys.dtype),
               pltpu.SemaphoreType.DMA((2,))]
    any_spec = pl.BlockSpec(memory_space=pl.ANY)
    if final_g is None:
        return pl.pallas_call(
            functools.partial(_combine_kernel, nt=nt),
            out_shape=jax.ShapeDtypeStruct((nb, l, d), F32),
            grid_spec=pltpu.PrefetchScalarGridSpec(
                num_scalar_prefetch=1, grid=(nb, nt),
                in_specs=[tok(d, t0), mod, tok(LANES, 0), any_spec],
                out_specs=tok(d, t0), scratch_shapes=scratch),
            input_output_aliases={1: 0},
            compiler_params=_cp("arbitrary", "arbitrary"),
            name="moe_combine",
        )(pos, h, mods, rw, ys)
    return pl.pallas_call(
        functools.partial(_combine_final_kernel, nt=nt),
        out_shape=jax.ShapeDtypeStruct((nb, nt * TM, d), F32),
        grid_spec=pltpu.PrefetchScalarGridSpec(
            num_scalar_prefetch=1, grid=(nb, nt),
            in_specs=[tok(d, t0), mod, tok(LANES, 0), pl.BlockSpec((1, d), lambda b, t, p: (0, 0)), any_spec],
            out_specs=tok(d, 0), scratch_shapes=scratch),
        compiler_params=_cp("arbitrary", "arbitrary"),
        name="moe_combine_final",
    )(pos, h, mods, rw, final_g.reshape(1, d), ys)


def _moe(h, a, ri, rw, counts, mods, n_experts, w1, w3, w2, ctx_tiles, t0, final_g=None):
    nb, n, d = a.shape
    tokens = nb * n
    n_tiles = (TOP_K * tokens) // TM + n_experts
    counts = counts[0, :n_experts]
    padded = ((counts + TM - 1) // TM) * TM
    ends = jnp.cumsum(padded)
    starts = ends - padded
    ri = ri.reshape(tokens, LANES)
    pos = (starts[ri[:, 0:TOP_K]] + ri[:, TOP_K:2 * TOP_K]).reshape(TOP_K * tokens)
    n_valid = ends[-1] // TM
    x_index = jnp.maximum(jnp.minimum(jnp.arange(n_tiles, dtype=jnp.int32), n_valid - 1), 0)
    tile_expert = jnp.minimum(jnp.sum(ends[None, :] <= (x_index * TM)[:, None], axis=1), n_experts - 1)
    fill = jnp.arange(TM, dtype=jnp.int32)[None, :]
    pad_rows = jnp.where(fill < (padded - counts)[:, None], (starts + counts)[:, None] + fill, -1)
    n_valid = n_valid.reshape(1).astype(jnp.int32)
    xs = _moe_dispatch(a, pos.astype(jnp.int32),
                       pad_rows.reshape(-1).astype(jnp.int32), n_valid, n_tiles * TM)
    ys = _moe_experts(xs, tile_expert.astype(jnp.int32), x_index, n_valid, w1, w3, w2, n_tiles)
    return _moe_combine(ys, pos.astype(jnp.int32), rw, h, mods, ctx_tiles, t0, final_g)


def kernel(x, c, ctx, c_ctx, ada_w, ada_b, norm_mix_g, norm_ffn_g, final_g, conv_pw1_w, conv_pw1_b, conv_dw_w, conv_dw_b, conv_ln_g, conv_ln_b, conv_pw2_w, conv_pw2_b, rec_w_in, rec_lb_logits, rec_onorm_g, rec_w_o, ffn_w1, ffn_w3, ffn_w2, moe_router, moe_w1, moe_w3, moe_w2):
    nb, seq, d = x.shape
    n_ctx = ctx.shape[1]
    depth = ada_w.shape[0]
    ctx_tiles = n_ctx // TM
    assert n_ctx % TM == 0 and seq % TM == 0 and n_ctx % BLK == 0 and seq % (GRID_W * 8) == 0
    assert d % (2 * 256) == 0 and nb < 16 and depth % 2 == 0

    cond = jnp.zeros((16, d), F32).at[:nb].set(c).at[nb].set(c_ctx)
    mods = _modulation(cond, ada_w, ada_b)
    sm = jax.nn.softmax(rec_lb_logits.astype(F32), axis=1)
    lower = jnp.cumsum(sm, axis=1) - sm[:, :1]

    bf = lambda w: w.astype(BF16)
    conv_pw1_w, conv_pw2_w, rec_w_in, rec_w_o = bf(conv_pw1_w), bf(conv_pw2_w), bf(rec_w_in), bf(rec_w_o)
    ffn_w1, ffn_w3, ffn_w2 = bf(ffn_w1), bf(ffn_w3), bf(ffn_w2)

    h = jnp.concatenate([ctx, x], axis=1)
    for i in range(depth):
        last = i == depth - 1
        j = i // 2
        t0 = ctx_tiles if last else 0
        if i % 2 == 0:
            u = _pw1_glu(h, mods[i], norm_mix_g[i], conv_pw1_w, j, conv_pw1_b[j], ctx_tiles)
            v = _dwconv(u, conv_dw_w[j], conv_dw_b[j], n_ctx)
            h, e1, e3, e2 = _pw2_ffn(v, h, mods[i], conv_ln_g[j], conv_ln_b[j], conv_pw2_w, j, conv_pw2_b[j],
                                     norm_ffn_g[i], ffn_w1, ffn_w3, ffn_w2, moe_w1, moe_w3, moe_w2, ctx_tiles)
        else:
            p = _rec_in_proj(h, mods[i], norm_mix_g[i], rec_w_in, j, lower[:, i], ctx_tiles)
            o_f, o_b = _gla(p, n_ctx)
            h, a, ri, rw, counts = _readout_route(o_f, o_b, p, h, mods[i], rec_onorm_g[j], rec_w_o, j,
                                                  norm_ffn_g[i], moe_router[j], ctx_tiles, t0)
            h = _moe(h, a, ri, rw, counts, mods[i], moe_router.shape[-1], e1, e3, e2,
                     ctx_tiles, t0, final_g if last else None)
    return h
```

```python
import functools

import numpy as np
import jax
import jax.numpy as jnp
from jax import lax
from jax.experimental import pallas as pl
from jax.experimental.pallas import tpu as pltpu

EPS = 1e-6
GRID_W = 64
CONV_PAD = 16
HEAD_DIM = 128
BLK = 128
CHUNK = 16
TOP_K = 2
TM = 256
LANES = 128
VMEM_LIMIT = 56 * 1024 * 1024

F32 = jnp.float32
BF16 = jnp.bfloat16


def _cp(*sem):
    return pltpu.CompilerParams(dimension_semantics=sem, vmem_limit_bytes=VMEM_LIMIT)


def _dot(a, b):
    return jnp.dot(a, b, preferred_element_type=F32)


def _dot_nt(a, b):
    return lax.dot_general(a, b, (((1,), (1,)), ((), ())), preferred_element_type=F32)


def _dot_tn(a, b):
    return lax.dot_general(a, b, (((0,), (0,)), ((), ())), preferred_element_type=F32)


def _sigmoid(x):
    return 1.0 / (1.0 + jnp.exp(-x))


def _silu(x):
    return x * _sigmoid(x)


def _rms_mod(x, g, shift, scale):
    ms = jnp.mean(x * x, axis=-1, keepdims=True)
    return (x * lax.rsqrt(ms + EPS) * g) * (1.0 + scale) + shift


def _mod_kernel(cond_ref, w_ref, b_ref, o_ref):
    s = _silu(cond_ref[...])
    o_ref[0] = jnp.dot(s, w_ref[0], preferred_element_type=F32,
                       precision=lax.Precision.HIGHEST) + b_ref[0]


def _modulation(cond, ada_w, ada_b):
    depth, d, n = ada_w.shape
    tn = n // 4
    out = pl.pallas_call(
        _mod_kernel,
        out_shape=jax.ShapeDtypeStruct((depth, cond.shape[0], n), F32),
        grid=(depth, n // tn),
        in_specs=[
            pl.BlockSpec(cond.shape, lambda l, j: (0, 0)),
            pl.BlockSpec((1, d, tn), lambda l, j: (l, 0, j)),
            pl.BlockSpec((1, 1, tn), lambda l, j: (l, 0, j)),
        ],
        out_specs=pl.BlockSpec((1, cond.shape[0], tn), lambda l, j: (l, 0, j)),
        compiler_params=_cp("arbitrary", "arbitrary"),
        name="modulation",
    )(cond, ada_w, ada_b.reshape(depth, 1, n))
    return out.reshape(depth, cond.shape[0], 6, d)


def _row_specs(nb, d, t0, ctx_tiles):
    tok = lambda width: pl.BlockSpec((1, TM, width), lambda b, t: (b, t + t0, 0))
    mod = pl.BlockSpec((1, 6, d), lambda b, t: (jnp.where(t + t0 < ctx_tiles, nb, b), 0, 0))
    return tok, mod


def _full(shape):
    return pl.BlockSpec(shape, lambda b, t: (0,) * len(shape))


def _layer(shape, j):
    return pl.BlockSpec((None,) + shape, lambda b, t: (j,) + (0,) * len(shape),
                        pipeline_mode=pl.Buffered(1))


def _pw1_kernel(x_ref, mod_ref, g_ref, w_ref, b_ref, u_ref):
    m = mod_ref[0]
    a = _rms_mod(x_ref[0], g_ref[...], m[0:1], m[1:2]).astype(BF16)
    d = u_ref.shape[-1]
    cw = 512
    for c in range(0, d, cw):
        a1 = _dot(a, w_ref[:, c:c + cw]) + b_ref[:, c:c + cw]
        a2 = _dot(a, w_ref[:, d + c:d + c + cw]) + b_ref[:, d + c:d + c + cw]
        u_ref[0, :, c:c + cw] = a1 * _sigmoid(a2)


def _pw1_glu(h, mods, g, w, j, b, ctx_tiles):
    nb, l, d = h.shape
    tok, mod = _row_specs(nb, d, 0, ctx_tiles)
    return pl.pallas_call(
        _pw1_kernel,
        out_shape=jax.ShapeDtypeStruct((nb, l, d), F32),
        grid=(nb, l // TM),
        in_specs=[tok(d), mod, _full((1, d)), _layer((d, 2 * d), j), _full((1, 2 * d))],
        out_specs=tok(d),
        compiler_params=_cp("parallel", "arbitrary"),
        name="conv_pw1_glu",
    )(h, mods, g.reshape(1, d), w, b.reshape(1, 2 * d))


def _conv_kernel(u_ref, w_ref, b_ref, v_ref, pad_ref, *, ctx, rows, width_tiles):
    kw = w_ref.shape[0]
    half = (kw - 1) // 2
    cw = u_ref.shape[-1]
    j = pl.program_id(1)
    bias = jnp.broadcast_to(b_ref[...], (GRID_W, cw))
    zeros = jnp.zeros((CONV_PAD, cw), F32)

    def line_taps(win):
        acc = bias
        n = win.shape[0]
        shifted = {0: win}
        for k in range(kw):
            s = CONV_PAD + k - half
            b, a8 = s % 8, (s // 8) * 8
            if b not in shifted:
                shifted[b] = pltpu.roll(win, n - b, axis=0)
            acc = acc + w_ref[k:k + 1, :] * shifted[b][a8:a8 + GRID_W]
        return acc

    pad_ref[0:CONV_PAD] = zeros
    pad_ref[CONV_PAD:CONV_PAD + ctx] = u_ref[0, 0:ctx]
    pad_ref[CONV_PAD + ctx:2 * CONV_PAD + ctx] = zeros
    for c0 in range(0, ctx, GRID_W):
        v_ref[0, c0:c0 + GRID_W] = line_taps(pad_ref[c0:c0 + GRID_W + 2 * CONV_PAD])

    @pl.when(j < width_tiles)
    def _along_width():
        stride = GRID_W + CONV_PAD
        pad_ref[0:CONV_PAD] = zeros
        for r in range(rows):
            base = CONV_PAD + r * stride
            pad_ref[base:base + GRID_W] = u_ref[0, ctx + r * GRID_W:ctx + (r + 1) * GRID_W]
            pad_ref[base + GRID_W:base + stride] = zeros

        def body(r, carry):
            base = pl.multiple_of(r * stride, 8)
            win = pad_ref[pl.ds(base, GRID_W + 2 * CONV_PAD)]
            out0 = pl.multiple_of(ctx + r * GRID_W, 8)
            v_ref[0, pl.ds(out0, GRID_W)] = line_taps(win)
            return carry

        lax.fori_loop(0, rows, body, 0)

    @pl.when(j >= width_tiles)
    def _along_height():
        npad = half * GRID_W
        zrow = jnp.zeros((GRID_W, cw), F32)
        for r in range(half):
            pad_ref[r * GRID_W:(r + 1) * GRID_W] = zrow
            lo = npad + (rows + r) * GRID_W
            pad_ref[lo:lo + GRID_W] = zrow
        pad_ref[npad:npad + rows * GRID_W] = u_ref[0, ctx:ctx + rows * GRID_W]

        def body(r, carry):
            acc = bias
            for k in range(kw):
                src = pl.multiple_of((r + k) * GRID_W, 8)
                acc = acc + w_ref[k:k + 1, :] * pad_ref[pl.ds(src, GRID_W)]
            out0 = pl.multiple_of(ctx + r * GRID_W, 8)
            v_ref[0, pl.ds(out0, GRID_W)] = acc
            return carry

        lax.fori_loop(0, rows, body, 0)


def _dwconv(u, w, b, ctx):
    nb, l, d = u.shape
    kw = w.shape[0]
    cw = 256
    rows = (l - ctx) // GRID_W
    half = (kw - 1) // 2
    pad_rows = max(2 * CONV_PAD + ctx, CONV_PAD + rows * (GRID_W + CONV_PAD),
                   (rows + 2 * half) * GRID_W)
    kern = functools.partial(_conv_kernel, ctx=ctx, rows=rows, width_tiles=(d // 2) // cw)
    return pl.pallas_call(
        kern,
        out_shape=jax.ShapeDtypeStruct((nb, l, d), F32),
        grid=(nb, d // cw),
        in_specs=[
            pl.BlockSpec((1, l, cw), lambda b_, j: (b_, 0, j)),
            pl.BlockSpec((kw, cw), lambda b_, j: (0, j)),
            pl.BlockSpec((1, cw), lambda b_, j: (0, j)),
        ],
        out_specs=pl.BlockSpec((1, l, cw), lambda b_, j: (b_, 0, j)),
        scratch_shapes=[pltpu.VMEM((pad_rows, cw), F32)],
        compiler_params=_cp("parallel", "arbitrary"),
        name="axial_dwconv",
    )(u, w, b.reshape(1, d))


def _swiglu_hidden(a, w1_ref, w3_ref, hm_ref):
    f = hm_ref.shape[-1]
    cw = 256
    for c in range(0, f, cw):
        h1 = _dot(a, w1_ref[:, c:c + cw])
        h3 = _dot(a, w3_ref[:, c:c + cw])
        hm_ref[:, c:c + cw] = (_silu(h1) * h3).astype(BF16)


def _pw2_ffn_kernel(v_ref, h_ref, mod_ref, lng_ref, lnb_ref, pw_ref, pb_ref, g_ref, w1_ref, w3_ref, w2_ref,
                    e1_ref, e3_ref, e2_ref, o_ref, c1_ref, c3_ref, c2_ref, hm_ref):
    c1_ref[...] = e1_ref[...].astype(BF16)
    c3_ref[...] = e3_ref[...].astype(BF16)
    c2_ref[...] = e2_ref[...].astype(BF16)
    m = mod_ref[0]
    v = v_ref[0]
    mu = jnp.mean(v, axis=-1, keepdims=True)
    vc = v - mu
    var = jnp.mean(vc * vc, axis=-1, keepdims=True)
    y = _silu(vc * lax.rsqrt(var + EPS) * lng_ref[...] + lnb_ref[...]).astype(BF16)
    h = h_ref[0] + m[2:3] * (_dot(y, pw_ref[...]) + pb_ref[...])
    a = _rms_mod(h, g_ref[...], m[3:4], m[4:5]).astype(BF16)
    _swiglu_hidden(a, w1_ref, w3_ref, hm_ref)
    o_ref[0] = h + m[5:6] * _dot(hm_ref[...], w2_ref[...])


EXPERT_CAST_SLABS = 8


def _pw2_ffn(v, h, mods, ln_g, ln_b, pw, j, pb, g, w1, w3, w2, e1, e3, e2, ctx_tiles):
    nb, l, d = h.shape
    f = w1.shape[-1]
    nt = l // TM
    ne = e1.shape[1]
    slabs = ne * EXPERT_CAST_SLABS
    assert nb * nt >= slabs and d % (16 * EXPERT_CAST_SLABS) == 0 and f % (16 * EXPERT_CAST_SLABS) == 0
    tok, mod = _row_specs(nb, d, 0, ctx_tiles)

    def slab(rows, cols, layer):
        def index(b, t):
            s = jnp.minimum(b * nt + t, slabs - 1)
            lead = (s // EXPERT_CAST_SLABS, s % EXPERT_CAST_SLABS, 0)
            return ((j,) + lead) if layer else lead
        shape = (1, rows // EXPERT_CAST_SLABS, cols)
        return pl.BlockSpec(((None,) + shape) if layer else shape, index)

    cast = lambda rows, cols: jax.ShapeDtypeStruct((ne, rows, cols), BF16)
    return pl.pallas_call(
        _pw2_ffn_kernel,
        out_shape=(jax.ShapeDtypeStruct((nb, l, d), F32), cast(d, f), cast(d, f), cast(f, d)),
        grid=(nb, nt),
        in_specs=[tok(d), tok(d), mod, _full((1, d)), _full((1, d)), _layer((d, d), j), _full((1, d)),
                  _full((1, d)), _layer((d, f), j), _layer((d, f), j), _layer((f, d), j),
                  slab(d, f, True), slab(d, f, True), slab(f, d, True)],
        out_specs=(tok(d), slab(d, f, False), slab(d, f, False), slab(f, d, False)),
        scratch_shapes=[pltpu.VMEM((TM, f), BF16)],
        input_output_aliases={1: 0},
        compiler_params=_cp("arbitrary", "arbitrary"),
        name="conv_pw2_ffn",
    )(v, h, mods, ln_g.reshape(1, d), ln_b.reshape(1, d), pw, pb.reshape(1, d), g.reshape(1, d), w1, w3, w2,
      e1, e3, e2)


def _win_kernel(x_ref, mod_ref, g_ref, w_ref, lb_ref, p_ref):
    m = mod_ref[0]
    a = _rms_mod(x_ref[0], g_ref[...], m[0:1], m[1:2]).astype(BF16)
    d = x_ref.shape[-1]
    cw = 512
    for c in range(0, 5 * d, cw):
        part, off = divmod(c, d)
        y = _dot(a, w_ref[:, c:c + cw])
        if part in (1, 2):
            lb = lb_ref[part - 1:part, off:off + cw]
            y = jnp.log(lb + (1.0 - lb) * _sigmoid(y))
        elif part == 3:
            y = _silu(y) * (HEAD_DIM ** -0.5)
        elif part == 4:
            y = _silu(y)
        p_ref[0, :, c:c + cw] = y


def _rec_in_proj(h, mods, g, w, j, lower, ctx_tiles):
    nb, l, d = h.shape
    n = w.shape[-1]
    tok, mod = _row_specs(nb, d, 0, ctx_tiles)
    return pl.pallas_call(
        _win_kernel,
        out_shape=jax.ShapeDtypeStruct((nb, l, n), F32),
        grid=(nb, l // TM),
        in_specs=[tok(d), mod, _full((1, d)), _layer((d, n), j), _full((2, d))],
        out_specs=tok(n),
        compiler_params=_cp("parallel", "arbitrary"),
        name="rec_in_proj",
    )(h, mods, g.reshape(1, d), w, lower)


N_CHUNKS = BLK // CHUNK
N_CHUNK_SUMS = 7
SUM_ROWS = BLK + N_CHUNK_SUMS * N_CHUNKS + 8


def _rec_constants():
    t = np.arange(BLK)
    mats, masks = [], []
    for direction in range(2):
        p = t if direction == 0 else BLK - 1 - t
        pt, ps = p[:, None], p[None, :]
        ct, cs = pt // CHUNK, ps // CHUNK
        same = ct == cs
        mid4 = N_CHUNKS // 2
        mid2 = (ct // 4) * 4 + 2
        chunk_level = [
            same,
            cs < ct,
            cs > ct,
            (ct >= mid4) & (cs >= mid4) & (cs < ct),
            (ct < mid4) & (cs > ct) & (cs < mid4),
            (ct >= mid2) & (cs >= mid2) & (cs < ct),
            (ct < mid2) & (cs > ct) & (cs < mid2),
        ]
        rows = [same & (ps <= pt)]
        for m in chunk_level:
            m = np.broadcast_to(m, (BLK, BLK))
            rows.append(m[::CHUNK])
        rows.append(np.ones((8, BLK), bool))
        mats.append(np.concatenate([r.astype(np.float32) for r in rows], axis=0))
        m = [
            same & (ps <= pt),
            (ct % 2 == 1) & (cs == ct - 1),
            (ct // 4 == cs // 4) & (ct % 4 >= 2) & (cs % 4 < 2),
            (ct >= mid4) & (cs < mid4),
        ]
        level = np.full((BLK, BLK), float(len(m)), np.float32)
        for i, x in enumerate(m):
            level = np.where(np.broadcast_to(x, (BLK, BLK)), np.float32(i), level)
        masks.append(level)
    return np.stack(mats), np.stack(masks)


def _chunk_scale(x, ec, i):
    base = i * N_CHUNKS
    parts = []
    for j in range(N_CHUNKS):
        e = jnp.broadcast_to(ec[base + j:base + j + 1], (CHUNK, x.shape[1]))
        parts.append(x[CHUNK * j:CHUNK * (j + 1)] * e)
    return jnp.concatenate(parts, axis=0)


def _gla_kernel(v_ref, lff_ref, lfb_ref, q_ref, a_ref, m_ref, of_ref, ob_ref, *, ctx_blocks, n_blocks):
    last = n_blocks - 1
    n_heads = v_ref.shape[-1] // HEAD_DIM
    chains = [(dr, hd) for dr in range(2) for hd in range(n_heads)]
    lf_refs = (lff_ref, lfb_ref)
    o_refs = (of_ref, ob_ref)
    tot_row = N_CHUNK_SUMS * N_CHUNKS

    def block_rows(n):
        rev = jnp.where(n < ctx_blocks, ctx_blocks - 1 - n, last + ctx_blocks - n)
        return pl.multiple_of(n * BLK, BLK), pl.multiple_of(rev * BLK, BLK)

    def decay_sums(row0):
        sums = []
        for dr in range(2):
            logf = lf_refs[dr][0, pl.ds(row0[dr], BLK), :]
            hi = logf.astype(BF16)
            lo = (logf - hi.astype(F32)).astype(BF16)
            sums.append((logf, _dot(a_ref[dr], jnp.concatenate([hi, lo], axis=0))))
        return sums

    def operands(row0, sums):
        ops = []
        for dr, hd in chains:
            cols = slice(hd * HEAD_DIM, (hd + 1) * HEAD_DIM)
            logf, s = sums[dr]
            s = s[:, cols]
            k = 1.0 - jnp.exp(logf[:, cols])
            qs = q_ref[0, pl.ds(row0[dr], BLK), cols]
            bcum = s[0:BLK]
            ec = jnp.exp(s[BLK:SUM_ROWS])
            q_in = qs * jnp.exp(bcum)
            k_in = k * jnp.exp(-bcum)
            k_st = _chunk_scale(k_in, ec, 0)
            ops.append(dict(
                q_blk=_chunk_scale(q_in, ec, 1).astype(BF16),
                k_end=_chunk_scale(k_st, ec, 2).astype(BF16),
                q4=_chunk_scale(q_in, ec, 3).astype(BF16),
                k4=_chunk_scale(k_st, ec, 4).astype(BF16),
                q2=_chunk_scale(q_in, ec, 5).astype(BF16),
                k2=_chunk_scale(k_st, ec, 6).astype(BF16),
                e_tot=ec[tot_row:tot_row + 1],
                q_in=q_in.astype(BF16), k_in=k_in.astype(BF16), k_st=k_st.astype(BF16),
                vb=v_ref[0, pl.ds(row0[dr], BLK), cols].astype(BF16)))
        return ops

    def products(ops, states):
        return [(_dot_nt(o["q_in"], o["k_in"]), _dot_nt(o["q_in"], o["k_st"]),
                 _dot_nt(o["q2"], o["k2"]), _dot_nt(o["q4"], o["k4"]),
                 _dot_nt(o["q_blk"], st.astype(BF16)), _dot_tn(o["vb"], o["k_end"]))
                for o, st in zip(ops, states)]

    def finish(row0, ops, raw, states):
        atts = []
        for (dr, hd), r in zip(chains, raw):
            level = m_ref[dr]
            att = jnp.where(level == 0.0, r[0], jnp.where(level == 1.0, r[1], jnp.where(
                level == 2.0, r[2], jnp.where(level == 3.0, r[3], 0.0))))
            atts.append(att.astype(BF16))
        new_states = []
        for (dr, hd), o, r, att, st in zip(chains, ops, raw, atts, states):
            cols = slice(hd * HEAD_DIM, (hd + 1) * HEAD_DIM)
            o_refs[dr][0, pl.ds(row0[dr], BLK), cols] = (_dot(att, o["vb"]) + r[4]).astype(o_refs[dr].dtype)
            new_states.append(o["e_tot"] * st + r[5])
        return tuple(new_states)

    def step(n, states):
        row0 = block_rows(n)
        ops = operands(row0, decay_sums(row0))
        return finish(row0, ops, products(ops, states), states)

    zero = jnp.zeros((HEAD_DIM, HEAD_DIM), F32)
    lax.fori_loop(0, n_blocks, step, (zero,) * len(chains))


GLA_HEADS_PER_STEP = 4


def _gla(p, ctx):
    nb, l, n5 = p.shape
    d = n5 // 5
    cw = GLA_HEADS_PER_STEP * HEAD_DIM
    groups = d // cw
    amat, masks = _rec_constants()
    amat = np.concatenate([amat, amat], axis=2)
    kern = functools.partial(_gla_kernel, ctx_blocks=ctx // BLK, n_blocks=l // BLK)
    col = lambda base: pl.BlockSpec((1, l, cw), lambda b, h: (b, 0, base + h))
    out = jax.ShapeDtypeStruct((nb, l, d), BF16)
    return pl.pallas_call(
        kern,
        out_shape=(out, out),
        grid=(nb, groups),
        in_specs=[col(0), col(groups), col(2 * groups), col(3 * groups),
                  _full(amat.shape), _full(masks.shape)],
        out_specs=(col(0), col(0)),
        compiler_params=_cp("parallel", "arbitrary"),
        name="hgrn2_gla",
    )(p, p, p, p, jnp.asarray(amat, BF16), jnp.asarray(masks, F32))


def _route_kernel(of_ref, ob_ref, gate_ref, h_ref, mod_ref, ng_ref, wo_ref, g_ref, r_ref, tri_ref,
                  hn_ref, a_ref, ri_ref, rw_ref, cnt_ref, carry_ref, y_ref, *, n_experts):
    @pl.when((pl.program_id(0) == 0) & (pl.program_id(1) == 0))
    def _():
        carry_ref[...] = jnp.zeros_like(carry_ref)

    m = mod_ref[0]
    d = of_ref.shape[-1]
    for c in range(0, d, HEAD_DIM):
        o = of_ref[0, :, c:c + HEAD_DIM].astype(F32) + ob_ref[0, :, c:c + HEAD_DIM].astype(F32)
        o = o * lax.rsqrt(jnp.mean(o * o, axis=-1, keepdims=True) + EPS) * ng_ref[:, c:c + HEAD_DIM]
        y_ref[:, c:c + HEAD_DIM] = (o * gate_ref[0, :, c:c + HEAD_DIM]).astype(BF16)
    h = h_ref[0] + m[2:3] * _dot(y_ref[...], wo_ref[...])
    hn_ref[0] = h
    a = _rms_mod(h, g_ref[...], m[3:4], m[4:5])
    a_ref[0] = a
    a_hi = a.astype(BF16)
    a_lo = (a - a_hi.astype(F32)).astype(BF16)
    part = _dot(a_hi, r_ref[...])
    logits = part[:, :LANES] + part[:, LANES:] + _dot(a_lo, r_ref[:, :LANES])
    lane = lax.broadcasted_iota(jnp.int32, logits.shape, 1)
    neg = jnp.float32(-jnp.inf)
    logits = jnp.where(lane < n_experts, logits, neg)
    v1 = jnp.max(logits, axis=-1, keepdims=True)
    i1 = jnp.min(jnp.where(logits == v1, lane, LANES), axis=-1, keepdims=True)
    rest = jnp.where(lane == i1, neg, logits)
    v2 = jnp.max(rest, axis=-1, keepdims=True)
    i2 = jnp.min(jnp.where(rest == v2, lane, LANES), axis=-1, keepdims=True)
    e2 = jnp.exp(v2 - v1)
    w1 = 1.0 / (1.0 + e2)
    w2 = e2 / (1.0 + e2)
    oh1 = jnp.where(lane == i1, 1.0, 0.0)
    oh2 = jnp.where(lane == i2, 1.0, 0.0)
    both = oh1 + oh2
    earlier = _dot(tri_ref[...], both.astype(BF16)) + carry_ref[...]
    r1 = jnp.sum(oh1 * earlier, axis=-1, keepdims=True).astype(jnp.int32)
    r2 = jnp.sum(oh2 * earlier, axis=-1, keepdims=True).astype(jnp.int32)
    carry = carry_ref[...] + jnp.sum(both, axis=0, keepdims=True)
    carry_ref[...] = carry
    cnt_ref[...] = jnp.broadcast_to(carry, cnt_ref.shape).astype(jnp.int32)
    ri_ref[0] = jnp.where(lane == 0, i1, jnp.where(lane == 1, i2, jnp.where(lane == 2, r1,
                          jnp.where(lane == 3, r2, 0))))
    rw_ref[0] = jnp.where(lane == 0, w1, jnp.where(lane == 1, w2, 0.0))


def _readout_route(o_f, o_b, p, h, mods, onorm_g, w_o, j, g, router, ctx_tiles, t0):
    nb, l, d = h.shape
    n_experts = router.shape[-1]
    nt = l // TM - t0
    tok, mod = _row_specs(nb, d, t0, ctx_tiles)
    gate_cols = pl.BlockSpec((1, TM, d), lambda b, t: (b, t + t0, 4))
    out_tok = lambda width: pl.BlockSpec((1, TM, width), lambda b, t: (b, t, 0))
    rpad = jnp.pad(router, ((0, 0), (0, LANES - n_experts)))
    r_hi = rpad.astype(BF16)
    r_split = jnp.concatenate([r_hi, (rpad - r_hi.astype(F32)).astype(BF16)], axis=1)
    tri = jnp.asarray(np.tril(np.ones((TM, TM), np.float32), -1), BF16)
    return pl.pallas_call(
        functools.partial(_route_kernel, n_experts=n_experts),
        out_shape=(jax.ShapeDtypeStruct((nb, l, d), F32),
                   jax.ShapeDtypeStruct((nb, nt * TM, d), F32),
                   jax.ShapeDtypeStruct((nb, nt * TM, LANES), jnp.int32),
                   jax.ShapeDtypeStruct((nb, nt * TM, LANES), F32),
                   jax.ShapeDtypeStruct((8, LANES), jnp.int32)),
        grid=(nb, nt),
        in_specs=[tok(d), tok(d), gate_cols, tok(d), mod, _full((1, d)), _layer((d, d), j),
                  _full((1, d)), _full((d, 2 * LANES)), _full((TM, TM))],
        out_specs=(tok(d), out_tok(d), out_tok(LANES), out_tok(LANES), _full((8, LANES))),
        scratch_shapes=[pltpu.VMEM((1, LANES), F32), pltpu.VMEM((TM, d), BF16)],
        input_output_aliases={3: 0},
        compiler_params=_cp("arbitrary", "arbitrary"),
        name="rec_readout_route",
    )(o_f, o_b, p, h, mods, onorm_g.reshape(1, d), w_o, g.reshape(1, d), r_split, tri)


def _row_copy(src, i, dst, j, sem):
    return pltpu.make_async_copy(src.at[pl.ds(i, 1)], dst.at[pl.ds(j, 1)], sem)


def _dispatch_kernel(pos_ref, pad_ref, nv_ref, a_ref, xs_hbm, zero_ref, sems, *, n_tok_tiles):
    step = pl.program_id(0)
    base = step * TM
    tile = a_ref.at[0]

    for r in range(TM):
        t = base + r
        _row_copy(tile, r, xs_hbm, pos_ref[2 * t], sems.at[0]).start(priority=0)
        _row_copy(tile, r, xs_hbm, pos_ref[2 * t + 1], sems.at[0]).start(priority=1)
    for _ in range(TOP_K):
        pltpu.make_async_copy(tile, xs_hbm.at[pl.ds(0, TM)], sems.at[0]).wait()

    @pl.when(step == n_tok_tiles - 1)
    def _fill():
        zero_ref[...] = jnp.zeros_like(zero_ref)

        def pad_start(i, carry):
            @pl.when(pad_ref[i] >= 0)
            def _():
                _row_copy(zero_ref, 0, xs_hbm, pad_ref[i], sems.at[1]).start()
            return carry

        def pad_wait(i, carry):
            @pl.when(pad_ref[i] >= 0)
            def _():
                _row_copy(zero_ref, 0, xs_hbm, 0, sems.at[1]).wait()
            return carry

        lax.fori_loop(0, pad_ref.shape[0], pad_start, 0)
        lax.fori_loop(0, pad_ref.shape[0], pad_wait, 0)

        n_tiles = xs_hbm.shape[0] // TM
        n_tail = n_tiles - TOP_K * n_tok_tiles

        def tail_copy(i):
            row0 = pl.multiple_of((nv_ref[0] + i) * TM, TM)
            return pltpu.make_async_copy(zero_ref, xs_hbm.at[pl.ds(row0, TM)], sems.at[1])

        def tail_start(i, carry):
            @pl.when(nv_ref[0] + i < n_tiles)
            def _():
                tail_copy(i).start()
            return carry

        def tail_wait(i, carry):
            @pl.when(nv_ref[0] + i < n_tiles)
            def _():
                tail_copy(i).wait()
            return carry

        lax.fori_loop(0, n_tail, tail_start, 0)
        lax.fori_loop(0, n_tail, tail_wait, 0)


def _moe_dispatch(a, pos, pad_rows, n_valid, n_rows):
    nb, n, d = a.shape
    nt = n // TM
    kern = functools.partial(_dispatch_kernel, n_tok_tiles=nb * nt)
    return pl.pallas_call(
        kern,
        out_shape=jax.ShapeDtypeStruct((n_rows, d), a.dtype),
        grid_spec=pltpu.PrefetchScalarGridSpec(
            num_scalar_prefetch=3, grid=(nb * nt,),
            in_specs=[pl.BlockSpec((1, TM, d), lambda i, p, q, v: (i // nt, i % nt, 0))],
            out_specs=pl.BlockSpec(memory_space=pl.ANY),
            scratch_shapes=[pltpu.VMEM((TM, d), a.dtype), pltpu.SemaphoreType.DMA((2,))]),
        compiler_params=_cp("arbitrary"),
        name="moe_dispatch",
    )(pos, pad_rows, n_valid, a)


def _experts_kernel(te_ref, xi_ref, nv_ref, x_ref, w1_ref, w3_ref, w2_ref, y_ref, hm_ref):
    @pl.when(pl.program_id(0) < nv_ref[0])
    def _():
        _swiglu_hidden(x_ref[...].astype(BF16), w1_ref.at[0], w3_ref.at[0], hm_ref)
        y_ref[...] = _dot(hm_ref[...], w2_ref[0])

    @pl.when(pl.program_id(0) >= nv_ref[0])
    def _():
        y_ref[...] = jnp.zeros_like(y_ref)


def _moe_experts(xs, tile_expert, x_index, n_valid, w1, w3, w2, n_tiles):
    dp = xs.shape[-1]
    d, f = w1.shape[-2:]
    assert dp == d
    rows = pl.BlockSpec((TM, dp), lambda i, te, xi, nv: (xi[i], 0))
    wspec = lambda s: pl.BlockSpec((1,) + s, lambda i, te, xi, nv: (te[i], 0, 0))
    return pl.pallas_call(
        _experts_kernel,
        out_shape=jax.ShapeDtypeStruct((n_tiles * TM, dp), xs.dtype),
        grid_spec=pltpu.PrefetchScalarGridSpec(
            num_scalar_prefetch=3, grid=(n_tiles,),
            in_specs=[rows, wspec((d, f)), wspec((d, f)), wspec((f, d))],
            out_specs=pl.BlockSpec((TM, dp), lambda i, te, xi, nv: (i, 0)),
            scratch_shapes=[pltpu.VMEM((TM, f), BF16)]),
        compiler_params=_cp("arbitrary"),
        name="moe_experts",
    )(tile_expert, x_index, n_valid, xs, w1, w3, w2)


def _combine_kernel(pos_ref, h_ref, mod_ref, rw_ref, ys_hbm, o_ref, y1_ref, y2_ref, sems, *, nt, fg_ref=None):
    base = (pl.program_id(0) * nt + pl.program_id(1)) * TM

    for r in range(TM):
        t = base + r
        _row_copy(ys_hbm, pos_ref[2 * t], y1_ref, r, sems.at[0]).start(priority=0)
        _row_copy(ys_hbm, pos_ref[2 * t + 1], y2_ref, r, sems.at[1]).start(priority=1)
    pltpu.make_async_copy(ys_hbm.at[pl.ds(0, TM)], y1_ref, sems.at[0]).wait()
    pltpu.make_async_copy(ys_hbm.at[pl.ds(0, TM)], y2_ref, sems.at[1]).wait()
    rw = rw_ref[0]
    y = rw[:, 0:1] * y1_ref[...] + rw[:, 1:2] * y2_ref[...]
    out = h_ref[0] + mod_ref[0][5:6] * y
    if fg_ref is not None:
        out = out * lax.rsqrt(jnp.mean(out * out, axis=-1, keepdims=True) + EPS) * fg_ref[...]
    o_ref[0] = out


def _combine_final_kernel(pos_ref, h_ref, mod_ref, rw_ref, fg_ref, ys_hbm, o_ref, y1_ref, y2_ref, sems, *, nt):
    _combine_kernel(pos_ref, h_ref, mod_ref, rw_ref, ys_hbm, o_ref, y1_ref, y2_ref, sems, nt=nt, fg_ref=fg_ref)


def _moe_combine(ys, pos, rw, h, mods, ctx_tiles, t0, final_g=None):
    nb, l, d = h.shape
    nt = l // TM - t0
    tok = lambda width, off: pl.BlockSpec((1, TM, width), lambda b, t, p: (b, t + off, 0))
    mod = pl.BlockSpec((1, 6, d), lambda b, t, p: (jnp.where(t + t0 < ctx_tiles, nb, b), 0, 0))
    scratch = [pltpu.VMEM((TM, ys.shape[-1]), ys.dtype), pltpu.VMEM((TM, ys.shape[-1]), ys.dtype),
               pltpu.SemaphoreType.DMA((2,))]
    any_spec = pl.BlockSpec(memory_space=pl.ANY)
    if final_g is None:
        return pl.pallas_call(
            functools.partial(_combine_kernel, nt=nt),
            out_shape=jax.ShapeDtypeStruct((nb, l, d), F32),
            grid_spec=pltpu.PrefetchScalarGridSpec(
                num_scalar_prefetch=1, grid=(nb, nt),
                in_specs=[tok(d, t0), mod, tok(LANES, 0), any_spec],
                out_specs=tok(d, t0), scratch_shapes=scratch),
            input_output_aliases={1: 0},
            compiler_params=_cp("arbitrary", "arbitrary"),
            name="moe_combine",
        )(pos, h, mods, rw, ys)
    return pl.pallas_call(
        functools.partial(_combine_final_kernel, nt=nt),
        out_shape=jax.ShapeDtypeStruct((nb, nt * TM, d), F32),
        grid_spec=pltpu.PrefetchScalarGridSpec(
            num_scalar_prefetch=1, grid=(nb, nt),
            in_specs=[tok(d, t0), mod, tok(LANES, 0), pl.BlockSpec((1, d), lambda b, t, p: (0, 0)), any_spec],
            out_specs=tok(d, 0), scratch_shapes=scratch),
        compiler_params=_cp("arbitrary", "arbitrary"),
        name="moe_combine_final",
    )(pos, h, mods, rw, final_g.reshape(1, d), ys)


def _moe(h, a, ri, rw, counts, mods, n_experts, w1, w3, w2, ctx_tiles, t0, final_g=None):
    nb, n, d = a.shape
    tokens = nb * n
    n_tiles = (TOP_K * tokens) // TM + n_experts
    counts = counts[0, :n_experts]
    padded = ((counts + TM - 1) // TM) * TM
    ends = jnp.cumsum(padded)
    starts = ends - padded
    ri = ri.reshape(tokens, LANES)
    pos = (starts[ri[:, 0:TOP_K]] + ri[:, TOP_K:2 * TOP_K]).reshape(TOP_K * tokens)
    n_valid = ends[-1] // TM
    x_index = jnp.maximum(jnp.minimum(jnp.arange(n_tiles, dtype=jnp.int32), n_valid - 1), 0)
    tile_expert = jnp.minimum(jnp.sum(ends[None, :] <= (x_index * TM)[:, None], axis=1), n_experts - 1)
    fill = jnp.arange(TM, dtype=jnp.int32)[None, :]
    pad_rows = jnp.where(fill < (padded - counts)[:, None], (starts + counts)[:, None] + fill, -1)
    n_valid = n_valid.reshape(1).astype(jnp.int32)
    xs = _moe_dispatch(a, pos.astype(jnp.int32),
                       pad_rows.reshape(-1).astype(jnp.int32), n_valid, n_tiles * TM)
    ys = _moe_experts(xs, tile_expert.astype(jnp.int32), x_index, n_valid, w1, w3, w2, n_tiles)
    return _moe_combine(ys, pos.astype(jnp.int32), rw, h, mods, ctx_tiles, t0, final_g)


def kernel(x, c, ctx, c_ctx, ada_w, ada_b, norm_mix_g, norm_ffn_g, final_g, conv_pw1_w, conv_pw1_b, conv_dw_w, conv_dw_b, conv_ln_g, conv_ln_b, conv_pw2_w, conv_pw2_b, rec_w_in, rec_lb_logits, rec_onorm_g, rec_w_o, ffn_w1, ffn_w3, ffn_w2, moe_router, moe_w1, moe_w3, moe_w2):
    nb, seq, d = x.shape
    n_ctx = ctx.shape[1]
    depth = ada_w.shape[0]
    ctx_tiles = n_ctx // TM
    assert n_ctx % TM == 0 and seq % TM == 0 and n_ctx % BLK == 0 and seq % (GRID_W * 8) == 0
    assert d % (2 * 256) == 0 and nb < 16 and depth % 2 == 0

    cond = jnp.zeros((16, d), F32).at[:nb].set(c).at[nb].set(c_ctx)
    mods = _modulation(cond, ada_w, ada_b)
    sm = jax.nn.softmax(rec_lb_logits.astype(F32), axis=1)
    lower = jnp.cumsum(sm, axis=1) - sm[:, :1]

    bf = lambda w: w.astype(BF16)
    conv_pw1_w, conv_pw2_w, rec_w_in, rec_w_o = bf(conv_pw1_w), bf(conv_pw2_w), bf(rec_w_in), bf(rec_w_o)
    ffn_w1, ffn_w3, ffn_w2 = bf(ffn_w1), bf(ffn_w3), bf(ffn_w2)

    h = jnp.concatenate([ctx, x], axis=1)
    for i in range(depth):
        last = i == depth - 1
        j = i // 2
        t0 = ctx_tiles if last else 0
        if i % 2 == 0:
            u = _pw1_glu(h, mods[i], norm_mix_g[i], conv_pw1_w, j, conv_pw1_b[j], ctx_tiles)
            v = _dwconv(u, conv_dw_w[j], conv_dw_b[j], n_ctx)
            h, e1, e3, e2 = _pw2_ffn(v, h, mods[i], conv_ln_g[j], conv_ln_b[j], conv_pw2_w, j, conv_pw2_b[j],
                                     norm_ffn_g[i], ffn_w1, ffn_w3, ffn_w2, moe_w1, moe_w3, moe_w2, ctx_tiles)
        else:
            p = _rec_in_proj(h, mods[i], norm_mix_g[i], rec_w_in, j, lower[:, i], ctx_tiles)
            o_f, o_b = _gla(p, n_ctx)
            h, a, ri, rw, counts = _readout_route(o_f, o_b, p, h, mods[i], rec_onorm_g[j], rec_w_o, j,
                                                  norm_ffn_g[i], moe_router[j], ctx_tiles, t0)
            h = _moe(h, a, ri, rw, counts, mods[i], moe_router.shape[-1], e1, e3, e2,
                     ctx_tiles, t0, final_g if last else None)
    return h
```

```python
import functools

import numpy as np
import jax
import jax.numpy as jnp
from jax import lax
from jax.experimental import pallas as pl
from jax.experimental.pallas import tpu as pltpu

EPS = 1e-6
GRID_W = 64
CONV_PAD = 16
HEAD_DIM = 128
BLK = 128
CHUNK = 16
TOP_K = 2
TM = 256
LANES = 128
VMEM_LIMIT = 56 * 1024 * 1024

F32 = jnp.float32
BF16 = jnp.bfloat16


def _cp(*sem):
    return pltpu.CompilerParams(dimension_semantics=sem, vmem_limit_bytes=VMEM_LIMIT)


def _dot(a, b):
    return jnp.dot(a, b, preferred_element_type=F32)


def _dot_nt(a, b):
    return lax.dot_general(a, b, (((1,), (1,)), ((), ())), preferred_element_type=F32)


def _dot_tn(a, b):
    return lax.dot_general(a, b, (((0,), (0,)), ((), ())), preferred_element_type=F32)


def _sigmoid(x):
    return 1.0 / (1.0 + jnp.exp(-x))


def _silu(x):
    return x * _sigmoid(x)


def _rms_mod(x, g, shift, scale):
    ms = jnp.mean(x * x, axis=-1, keepdims=True)
    return (x * lax.rsqrt(ms + EPS) * g) * (1.0 + scale) + shift


def _mod_kernel(cond_ref, w_ref, b_ref, o_ref):
    s = _silu(cond_ref[...])
    o_ref[0] = jnp.dot(s, w_ref[0], preferred_element_type=F32,
                       precision=lax.Precision.HIGHEST) + b_ref[0]


def _modulation(cond, ada_w, ada_b):
    depth, d, n = ada_w.shape
    tn = n // 4
    out = pl.pallas_call(
        _mod_kernel,
        out_shape=jax.ShapeDtypeStruct((depth, cond.shape[0], n), F32),
        grid=(depth, n // tn),
        in_specs=[
            pl.BlockSpec(cond.shape, lambda l, j: (0, 0)),
            pl.BlockSpec((1, d, tn), lambda l, j: (l, 0, j)),
            pl.BlockSpec((1, 1, tn), lambda l, j: (l, 0, j)),
        ],
        out_specs=pl.BlockSpec((1, cond.shape[0], tn), lambda l, j: (l, 0, j)),
        compiler_params=_cp("arbitrary", "arbitrary"),
        name="modulation",
    )(cond, ada_w, ada_b.reshape(depth, 1, n))
    return out.reshape(depth, cond.shape[0], 6, d)


def _row_specs(nb, d, t0, ctx_tiles):
    tok = lambda width: pl.BlockSpec((1, TM, width), lambda b, t: (b, t + t0, 0))
    mod = pl.BlockSpec((1, 6, d), lambda b, t: (jnp.where(t + t0 < ctx_tiles, nb, b), 0, 0))
    return tok, mod


def _full(shape):
    return pl.BlockSpec(shape, lambda b, t: (0,) * len(shape))


def _layer(shape, j):
    return pl.BlockSpec((None,) + shape, lambda b, t: (j,) + (0,) * len(shape),
                        pipeline_mode=pl.Buffered(1))


def _pw1_kernel(x_ref, mod_ref, g_ref, w_ref, b_ref, u_ref):
    m = mod_ref[0]
    a = _rms_mod(x_ref[0], g_ref[...], m[0:1], m[1:2]).astype(BF16)
    d = u_ref.shape[-1]
    cw = 512
    for c in range(0, d, cw):
        a1 = _dot(a, w_ref[:, c:c + cw]) + b_ref[:, c:c + cw]
        a2 = _dot(a, w_ref[:, d + c:d + c + cw]) + b_ref[:, d + c:d + c + cw]
        u_ref[0, :, c:c + cw] = a1 * _sigmoid(a2)


def _pw1_glu(h, mods, g, w, j, b, ctx_tiles):
    nb, l, d = h.shape
    tok, mod = _row_specs(nb, d, 0, ctx_tiles)
    return pl.pallas_call(
        _pw1_kernel,
        out_shape=jax.ShapeDtypeStruct((nb, l, d), F32),
        grid=(nb, l // TM),
        in_specs=[tok(d), mod, _full((1, d)), _layer((d, 2 * d), j), _full((1, 2 * d))],
        out_specs=tok(d),
        compiler_params=_cp("parallel", "arbitrary"),
        name="conv_pw1_glu",
    )(h, mods, g.reshape(1, d), w, b.reshape(1, 2 * d))


def _conv_kernel(u_ref, w_ref, b_ref, v_ref, pad_ref, *, ctx, rows, width_tiles):
    kw = w_ref.shape[0]
    half = (kw - 1) // 2
    cw = u_ref.shape[-1]
    j = pl.program_id(1)
    bias = jnp.broadcast_to(b_ref[...], (GRID_W, cw))
    zeros = jnp.zeros((CONV_PAD, cw), F32)

    def line_taps(win):
        acc = bias
        n = win.shape[0]
        shifted = {0: win}
        for k in range(kw):
            s = CONV_PAD + k - half
            b, a8 = s % 8, (s // 8) * 8
            if b not in shifted:
                shifted[b] = pltpu.roll(win, n - b, axis=0)
            acc = acc + w_ref[k:k + 1, :] * shifted[b][a8:a8 + GRID_W]
        return acc

    pad_ref[0:CONV_PAD] = zeros
    pad_ref[CONV_PAD:CONV_PAD + ctx] = u_ref[0, 0:ctx]
    pad_ref[CONV_PAD + ctx:2 * CONV_PAD + ctx] = zeros
    for c0 in range(0, ctx, GRID_W):
        v_ref[0, c0:c0 + GRID_W] = line_taps(pad_ref[c0:c0 + GRID_W + 2 * CONV_PAD])

    @pl.when(j < width_tiles)
    def _along_width():
        stride = GRID_W + CONV_PAD
        pad_ref[0:CONV_PAD] = zeros
        for r in range(rows):
            base = CONV_PAD + r * stride
            pad_ref[base:base + GRID_W] = u_ref[0, ctx + r * GRID_W:ctx + (r + 1) * GRID_W]
            pad_ref[base + GRID_W:base + stride] = zeros

        def body(r, carry):
            base = pl.multiple_of(r * stride, 8)
            win = pad_ref[pl.ds(base, GRID_W + 2 * CONV_PAD)]
            out0 = pl.multiple_of(ctx + r * GRID_W, 8)
            v_ref[0, pl.ds(out0, GRID_W)] = line_taps(win)
            return carry

        lax.fori_loop(0, rows, body, 0)

    @pl.when(j >= width_tiles)
    def _along_height():
        npad = half * GRID_W
        zrow = jnp.zeros((GRID_W, cw), F32)
        for r in range(half):
            pad_ref[r * GRID_W:(r + 1) * GRID_W] = zrow
            lo = npad + (rows + r) * GRID_W
            pad_ref[lo:lo + GRID_W] = zrow
        pad_ref[npad:npad + rows * GRID_W] = u_ref[0, ctx:ctx + rows * GRID_W]

        def body(r, carry):
            acc = bias
            for k in range(kw):
                src = pl.multiple_of((r + k) * GRID_W, 8)
                acc = acc + w_ref[k:k + 1, :] * pad_ref[pl.ds(src, GRID_W)]
            out0 = pl.multiple_of(ctx + r * GRID_W, 8)
            v_ref[0, pl.ds(out0, GRID_W)] = acc
            return carry

        lax.fori_loop(0, rows, body, 0)


def _dwconv(u, w, b, ctx):
    nb, l, d = u.shape
    kw = w.shape[0]
    cw = 256
    rows = (l - ctx) // GRID_W
    half = (kw - 1) // 2
    pad_rows = max(2 * CONV_PAD + ctx, CONV_PAD + rows * (GRID_W + CONV_PAD),
                   (rows + 2 * half) * GRID_W)
    kern = functools.partial(_conv_kernel, ctx=ctx, rows=rows, width_tiles=(d // 2) // cw)
    return pl.pallas_call(
        kern,
        out_shape=jax.ShapeDtypeStruct((nb, l, d), F32),
        grid=(nb, d // cw),
        in_specs=[
            pl.BlockSpec((1, l, cw), lambda b_, j: (b_, 0, j)),
            pl.BlockSpec((kw, cw), lambda b_, j: (0, j)),
            pl.BlockSpec((1, cw), lambda b_, j: (0, j)),
        ],
        out_specs=pl.BlockSpec((1, l, cw), lambda b_, j: (b_, 0, j)),
        scratch_shapes=[pltpu.VMEM((pad_rows, cw), F32)],
        compiler_params=_cp("parallel", "arbitrary"),
        name="axial_dwconv",
    )(u, w, b.reshape(1, d))


def _swiglu_hidden(a, w1_ref, w3_ref, hm_ref):
    f = hm_ref.shape[-1]
    cw = 256
    for c in range(0, f, cw):
        h1 = _dot(a, w1_ref[:, c:c + cw])
        h3 = _dot(a, w3_ref[:, c:c + cw])
        hm_ref[:, c:c + cw] = (_silu(h1) * h3).astype(BF16)


def _pw2_ffn_kernel(v_ref, h_ref, mod_ref, lng_ref, lnb_ref, pw_ref, pb_ref, g_ref, w1_ref, w3_ref, w2_ref,
                    e1_ref, e3_ref, e2_ref, o_ref, c1_ref, c3_ref, c2_ref, hm_ref):
    c1_ref[...] = e1_ref[...].astype(BF16)
    c3_ref[...] = e3_ref[...].astype(BF16)
    c2_ref[...] = e2_ref[...].astype(BF16)
    m = mod_ref[0]
    v = v_ref[0]
    mu = jnp.mean(v, axis=-1, keepdims=True)
    vc = v - mu
    var = jnp.mean(vc * vc, axis=-1, keepdims=True)
    y = _silu(vc * lax.rsqrt(var + EPS) * lng_ref[...] + lnb_ref[...]).astype(BF16)
    h = h_ref[0] + m[2:3] * (_dot(y, pw_ref[...]) + pb_ref[...])
    a = _rms_mod(h, g_ref[...], m[3:4], m[4:5]).astype(BF16)
    _swiglu_hidden(a, w1_ref, w3_ref, hm_ref)
    o_ref[0] = h + m[5:6] * _dot(hm_ref[...], w2_ref[...])


EXPERT_CAST_SLABS = 8


def _pw2_ffn(v, h, mods, ln_g, ln_b, pw, j, pb, g, w1, w3, w2, e1, e3, e2, ctx_tiles):
    nb, l, d = h.shape
    f = w1.shape[-1]
    nt = l // TM
    ne = e1.shape[1]
    slabs = ne * EXPERT_CAST_SLABS
    assert nb * nt >= slabs and d % (16 * EXPERT_CAST_SLABS) == 0 and f % (16 * EXPERT_CAST_SLABS) == 0
    tok, mod = _row_specs(nb, d, 0, ctx_tiles)

    def slab(rows, cols, layer):
        def index(b, t):
            s = jnp.minimum(b * nt + t, slabs - 1)
            lead = (s // EXPERT_CAST_SLABS, s % EXPERT_CAST_SLABS, 0)
            return ((j,) + lead) if layer else lead
        shape = (1, rows // EXPERT_CAST_SLABS, cols)
        return pl.BlockSpec(((None,) + shape) if layer else shape, index)

    cast = lambda rows, cols: jax.ShapeDtypeStruct((ne, rows, cols), BF16)
    return pl.pallas_call(
        _pw2_ffn_kernel,
        out_shape=(jax.ShapeDtypeStruct((nb, l, d), F32), cast(d, f), cast(d, f), cast(f, d)),
        grid=(nb, nt),
        in_specs=[tok(d), tok(d), mod, _full((1, d)), _full((1, d)), _layer((d, d), j), _full((1, d)),
                  _full((1, d)), _layer((d, f), j), _layer((d, f), j), _layer((f, d), j),
                  slab(d, f, True), slab(d, f, True), slab(f, d, True)],
        out_specs=(tok(d), slab(d, f, False), slab(d, f, False), slab(f, d, False)),
        scratch_shapes=[pltpu.VMEM((TM, f), BF16)],
        input_output_aliases={1: 0},
        compiler_params=_cp("arbitrary", "arbitrary"),
        name="conv_pw2_ffn",
    )(v, h, mods, ln_g.reshape(1, d), ln_b.reshape(1, d), pw, pb.reshape(1, d), g.reshape(1, d), w1, w3, w2,
      e1, e3, e2)


def _win_kernel(x_ref, mod_ref, g_ref, w_ref, lb_ref, p_ref):
    m = mod_ref[0]
    a = _rms_mod(x_ref[0], g_ref[...], m[0:1], m[1:2]).astype(BF16)
    d = x_ref.shape[-1]
    cw = 512
    for c in range(0, 5 * d, cw):
        part, off = divmod(c, d)
        y = _dot(a, w_ref[:, c:c + cw])
        if part in (1, 2):
            lb = lb_ref[part - 1:part, off:off + cw]
            y = jnp.log(lb + (1.0 - lb) * _sigmoid(y))
        elif part == 3:
            y = _silu(y) * (HEAD_DIM ** -0.5)
        elif part == 4:
            y = _silu(y)
        p_ref[0, :, c:c + cw] = y


def _rec_in_proj(h, mods, g, w, j, lower, ctx_tiles):
    nb, l, d = h.shape
    n = w.shape[-1]
    tok, mod = _row_specs(nb, d, 0, ctx_tiles)
    return pl.pallas_call(
        _win_kernel,
        out_shape=jax.ShapeDtypeStruct((nb, l, n), F32),
        grid=(nb, l // TM),
        in_specs=[tok(d), mod, _full((1, d)), _layer((d, n), j), _full((2, d))],
        out_specs=tok(n),
        compiler_params=_cp("parallel", "arbitrary"),
        name="rec_in_proj",
    )(h, mods, g.reshape(1, d), w, lower)


N_CHUNKS = BLK // CHUNK
N_CHUNK_SUMS = 7
SUM_ROWS = BLK + N_CHUNK_SUMS * N_CHUNKS + 8


def _rec_constants():
    t = np.arange(BLK)
    mats, masks = [], []
    for direction in range(2):
        p = t if direction == 0 else BLK - 1 - t
        pt, ps = p[:, None], p[None, :]
        ct, cs = pt // CHUNK, ps // CHUNK
        same = ct == cs
        mid4 = N_CHUNKS // 2
        mid2 = (ct // 4) * 4 + 2
        chunk_level = [
            same,
            cs < ct,
            cs > ct,
            (ct >= mid4) & (cs >= mid4) & (cs < ct),
            (ct < mid4) & (cs > ct) & (cs < mid4),
            (ct >= mid2) & (cs >= mid2) & (cs < ct),
            (ct < mid2) & (cs > ct) & (cs < mid2),
        ]
        rows = [same & (ps <= pt)]
        for m in chunk_level:
            m = np.broadcast_to(m, (BLK, BLK))
            rows.append(m[::CHUNK])
        rows.append(np.ones((8, BLK), bool))
        mats.append(np.concatenate([r.astype(np.float32) for r in rows], axis=0))
        m = [
            same & (ps <= pt),
            (ct % 2 == 1) & (cs == ct - 1),
            (ct // 4 == cs // 4) & (ct % 4 >= 2) & (cs % 4 < 2),
            (ct >= mid4) & (cs < mid4),
        ]
        level = np.full((BLK, BLK), float(len(m)), np.float32)
        for i, x in enumerate(m):
            level = np.where(np.broadcast_to(x, (BLK, BLK)), np.float32(i), level)
        masks.append(level)
    return np.stack(mats), np.stack(masks)


def _chunk_scale(x, ec, i):
    base = i * N_CHUNKS
    parts = []
    for j in range(N_CHUNKS):
        e = jnp.broadcast_to(ec[base + j:base + j + 1], (CHUNK, x.shape[1]))
        parts.append(x[CHUNK * j:CHUNK * (j + 1)] * e)
    return jnp.concatenate(parts, axis=0)


def _gla_kernel(v_ref, lff_ref, lfb_ref, q_ref, a_ref, m_ref, of_ref, ob_ref, *, ctx_blocks, n_blocks):
    last = n_blocks - 1
    n_heads = v_ref.shape[-1] // HEAD_DIM
    chains = [(dr, hd) for dr in range(2) for hd in range(n_heads)]
    lf_refs = (lff_ref, lfb_ref)
    o_refs = (of_ref, ob_ref)
    tot_row = N_CHUNK_SUMS * N_CHUNKS

    def block_rows(n):
        rev = jnp.where(n < ctx_blocks, ctx_blocks - 1 - n, last + ctx_blocks - n)
        return pl.multiple_of(n * BLK, BLK), pl.multiple_of(rev * BLK, BLK)

    def decay_sums(row0):
        sums = []
        for dr in range(2):
            logf = lf_refs[dr][0, pl.ds(row0[dr], BLK), :]
            hi = logf.astype(BF16)
            lo = (logf - hi.astype(F32)).astype(BF16)
            sums.append((logf, _dot(a_ref[dr], jnp.concatenate([hi, lo], axis=0))))
        return sums

    def operands(row0, sums):
        ops = []
        for dr, hd in chains:
            cols = slice(hd * HEAD_DIM, (hd + 1) * HEAD_DIM)
            logf, s = sums[dr]
            s = s[:, cols]
            k = 1.0 - jnp.exp(logf[:, cols])
            qs = q_ref[0, pl.ds(row0[dr], BLK), cols]
            bcum = s[0:BLK]
            ec = jnp.exp(s[BLK:SUM_ROWS])
            q_in = qs * jnp.exp(bcum)
            k_in = k * jnp.exp(-bcum)
            k_st = _chunk_scale(k_in, ec, 0)
            ops.append(dict(
                q_blk=_chunk_scale(q_in, ec, 1).astype(BF16),
                k_end=_chunk_scale(k_st, ec, 2).astype(BF16),
                q4=_chunk_scale(q_in, ec, 3).astype(BF16),
                k4=_chunk_scale(k_st, ec, 4).astype(BF16),
                q2=_chunk_scale(q_in, ec, 5).astype(BF16),
                k2=_chunk_scale(k_st, ec, 6).astype(BF16),
                e_tot=ec[tot_row:tot_row + 1],
                q_in=q_in.astype(BF16), k_in=k_in.astype(BF16), k_st=k_st.astype(BF16),
                vb=v_ref[0, pl.ds(row0[dr], BLK), cols].astype(BF16)))
        return ops

    def products(ops, states):
        return [(_dot_nt(o["q_in"], o["k_in"]), _dot_nt(o["q_in"], o["k_st"]),
                 _dot_nt(o["q2"], o["k2"]), _dot_nt(o["q4"], o["k4"]),
                 _dot_nt(o["q_blk"], st.astype(BF16)), _dot_tn(o["vb"], o["k_end"]))
                for o, st in zip(ops, states)]

    def finish(row0, ops, raw, states):
        atts = []
        for (dr, hd), r in zip(chains, raw):
            level = m_ref[dr]
            att = jnp.where(level == 0.0, r[0], jnp.where(level == 1.0, r[1], jnp.where(
                level == 2.0, r[2], jnp.where(level == 3.0, r[3], 0.0))))
            atts.append(att.astype(BF16))
        new_states = []
        for (dr, hd), o, r, att, st in zip(chains, ops, raw, atts, states):
            cols = slice(hd * HEAD_DIM, (hd + 1) * HEAD_DIM)
            o_refs[dr][0, pl.ds(row0[dr], BLK), cols] = (_dot(att, o["vb"]) + r[4]).astype(o_refs[dr].dtype)
            new_states.append(o["e_tot"] * st + r[5])
        return tuple(new_states)

    def step(n, states):
        row0 = block_rows(n)
        ops = operands(row0, decay_sums(row0))
        return finish(row0, ops, products(ops, states), states)

    zero = jnp.zeros((HEAD_DIM, HEAD_DIM), F32)
    lax.fori_loop(0, n_blocks, step, (zero,) * len(chains))


GLA_HEADS_PER_STEP = 4


def _gla(p, ctx):
    nb, l, n5 = p.shape
    d = n5 // 5
    cw = GLA_HEADS_PER_STEP * HEAD_DIM
    groups = d // cw
    amat, masks = _rec_constants()
    amat = np.concatenate([amat, amat], axis=2)
    kern = functools.partial(_gla_kernel, ctx_blocks=ctx // BLK, n_blocks=l // BLK)
    col = lambda base: pl.BlockSpec((1, l, cw), lambda b, h: (b, 0, base + h))
    out = jax.ShapeDtypeStruct((nb, l, d), BF16)
    return pl.pallas_call(
        kern,
        out_shape=(out, out),
        grid=(nb, groups),
        in_specs=[col(0), col(groups), col(2 * groups), col(3 * groups),
                  _full(amat.shape), _full(masks.shape)],
        out_specs=(col(0), col(0)),
        compiler_params=_cp("parallel", "arbitrary"),
        name="hgrn2_gla",
    )(p, p, p, p, jnp.asarray(amat, BF16), jnp.asarray(masks, F32))


def _route_kernel(of_ref, ob_ref, gate_ref, h_ref, mod_ref, ng_ref, wo_ref, g_ref, r_ref, tri_ref,
                  hn_ref, a_ref, ri_ref, rw_ref, cnt_ref, carry_ref, y_ref, *, n_experts):
    @pl.when((pl.program_id(0) == 0) & (pl.program_id(1) == 0))
    def _():
        carry_ref[...] = jnp.zeros_like(carry_ref)

    m = mod_ref[0]
    d = of_ref.shape[-1]
    for c in range(0, d, HEAD_DIM):
        o = of_ref[0, :, c:c + HEAD_DIM].astype(F32) + ob_ref[0, :, c:c + HEAD_DIM].astype(F32)
        o = o * lax.rsqrt(jnp.mean(o * o, axis=-1, keepdims=True) + EPS) * ng_ref[:, c:c + HEAD_DIM]
        y_ref[:, c:c + HEAD_DIM] = (o * gate_ref[0, :, c:c + HEAD_DIM]).astype(BF16)
    h = h_ref[0] + m[2:3] * _dot(y_ref[...], wo_ref[...])
    hn_ref[0] = h
    a = _rms_mod(h, g_ref[...], m[3:4], m[4:5])
    a_ref[0] = a
    a_hi = a.astype(BF16)
    a_lo = (a - a_hi.astype(F32)).astype(BF16)
    part = _dot(a_hi, r_ref[...])
    logits = part[:, :LANES] + part[:, LANES:] + _dot(a_lo, r_ref[:, :LANES])
    lane = lax.broadcasted_iota(jnp.int32, logits.shape, 1)
    neg = jnp.float32(-jnp.inf)
    logits = jnp.where(lane < n_experts, logits, neg)
    v1 = jnp.max(logits, axis=-1, keepdims=True)
    i1 = jnp.min(jnp.where(logits == v1, lane, LANES), axis=-1, keepdims=True)
    rest = jnp.where(lane == i1, neg, logits)
    v2 = jnp.max(rest, axis=-1, keepdims=True)
    i2 = jnp.min(jnp.where(rest == v2, lane, LANES), axis=-1, keepdims=True)
    e2 = jnp.exp(v2 - v1)
    w1 = 1.0 / (1.0 + e2)
    w2 = e2 / (1.0 + e2)
    oh1 = jnp.where(lane == i1, 1.0, 0.0)
    oh2 = jnp.where(lane == i2, 1.0, 0.0)
    both = oh1 + oh2
    earlier = _dot(tri_ref[...], both.astype(BF16)) + carry_ref[...]
    r1 = jnp.sum(oh1 * earlier, axis=-1, keepdims=True).astype(jnp.int32)
    r2 = jnp.sum(oh2 * earlier, axis=-1, keepdims=True).astype(jnp.int32)
    carry = carry_ref[...] + jnp.sum(both, axis=0, keepdims=True)
    carry_ref[...] = carry
    cnt_ref[...] = jnp.broadcast_to(carry, cnt_ref.shape).astype(jnp.int32)
    ri_ref[0] = jnp.where(lane == 0, i1, jnp.where(lane == 1, i2, jnp.where(lane == 2, r1,
                          jnp.where(lane == 3, r2, 0))))
    rw_ref[0] = jnp.where(lane == 0, w1, jnp.where(lane == 1, w2, 0.0))


def _readout_route(o_f, o_b, p, h, mods, onorm_g, w_o, j, g, router, ctx_tiles, t0):
    nb, l, d = h.shape
    n_experts = router.shape[-1]
    nt = l // TM - t0
    tok, mod = _row_specs(nb, d, t0, ctx_tiles)
    gate_cols = pl.BlockSpec((1, TM, d), lambda b, t: (b, t + t0, 4))
    out_tok = lambda width: pl.BlockSpec((1, TM, width), lambda b, t: (b, t, 0))
    rpad = jnp.pad(router, ((0, 0), (0, LANES - n_experts)))
    r_hi = rpad.astype(BF16)
    r_split = jnp.concatenate([r_hi, (rpad - r_hi.astype(F32)).astype(BF16)], axis=1)
    tri = jnp.asarray(np.tril(np.ones((TM, TM), np.float32), -1), BF16)
    return pl.pallas_call(
        functools.partial(_route_kernel, n_experts=n_experts),
        out_shape=(jax.ShapeDtypeStruct((nb, l, d), F32),
                   jax.ShapeDtypeStruct((nb, nt * TM, d), F32),
                   jax.ShapeDtypeStruct((nb, nt * TM, LANES), jnp.int32),
                   jax.ShapeDtypeStruct((nb, nt * TM, LANES), F32),
                   jax.ShapeDtypeStruct((8, LANES), jnp.int32)),
        grid=(nb, nt),
        in_specs=[tok(d), tok(d), gate_cols, tok(d), mod, _full((1, d)), _layer((d, d), j),
                  _full((1, d)), _full((d, 2 * LANES)), _full((TM, TM))],
        out_specs=(tok(d), out_tok(d), out_tok(LANES), out_tok(LANES), _full((8, LANES))),
        scratch_shapes=[pltpu.VMEM((1, LANES), F32), pltpu.VMEM((TM, d), BF16)],
        input_output_aliases={3: 0},
        compiler_params=_cp("arbitrary", "arbitrary"),
        name="rec_readout_route",
    )(o_f, o_b, p, h, mods, onorm_g.reshape(1, d), w_o, g.reshape(1, d), r_split, tri)


def _row_copy(src, i, dst, j, sem):
    return pltpu.make_async_copy(src.at[pl.ds(i, 1)], dst.at[pl.ds(j, 1)], sem)


def _dispatch_kernel(pos_ref, pad_ref, nv_ref, a_ref, xs_hbm, zero_ref, sems, *, n_tok_tiles):
    step = pl.program_id(0)
    base = step * TM
    tile = a_ref.at[0]

    for r in range(TM):
        t = base + r
        _row_copy(tile, r, xs_hbm, pos_ref[2 * t], sems.at[0]).start(priority=0)
        _row_copy(tile, r, xs_hbm, pos_ref[2 * t + 1], sems.at[0]).start(priority=1)
    for _ in range(TOP_K):
        pltpu.make_async_copy(tile, xs_hbm.at[pl.ds(0, TM)], sems.at[0]).wait()

    @pl.when(step == n_tok_tiles - 1)
    def _fill():
        zero_ref[...] = jnp.zeros_like(zero_ref)

        def pad_start(i, carry):
            @pl.when(pad_ref[i] >= 0)
            def _():
                _row_copy(zero_ref, 0, xs_hbm, pad_ref[i], sems.at[1]).start()
            return carry

        def pad_wait(i, carry):
            @pl.when(pad_ref[i] >= 0)
            def _():
                _row_copy(zero_ref, 0, xs_hbm, 0, sems.at[1]).wait()
            return carry

        lax.fori_loop(0, pad_ref.shape[0], pad_start, 0)
        lax.fori_loop(0, pad_ref.shape[0], pad_wait, 0)

        n_tiles = xs_hbm.shape[0] // TM
        n_tail = n_tiles - TOP_K * n_tok_tiles

        def tail_copy(i):
            row0 = pl.multiple_of((nv_ref[0] + i) * TM, TM)
            return pltpu.make_async_copy(zero_ref, xs_hbm.at[pl.ds(row0, TM)], sems.at[1])

        def tail_start(i, carry):
            @pl.when(nv_ref[0] + i < n_tiles)
            def _():
                tail_copy(i).start()
            return carry

        def tail_wait(i, carry):
            @pl.when(nv_ref[0] + i < n_tiles)
            def _():
                tail_copy(i).wait()
            return carry

        lax.fori_loop(0, n_tail, tail_start, 0)
        lax.fori_loop(0, n_tail, tail_wait, 0)


def _moe_dispatch(a, pos, pad_rows, n_valid, n_rows):
    nb, n, d = a.shape
    nt = n // TM
    kern = functools.partial(_dispatch_kernel, n_tok_tiles=nb * nt)
    return pl.pallas_call(
        kern,
        out_shape=jax.ShapeDtypeStruct((n_rows, d), a.dtype),
        grid_spec=pltpu.PrefetchScalarGridSpec(
            num_scalar_prefetch=3, grid=(nb * nt,),
            in_specs=[pl.BlockSpec((1, TM, d), lambda i, p, q, v: (i // nt, i % nt, 0))],
            out_specs=pl.BlockSpec(memory_space=pl.ANY),
            scratch_shapes=[pltpu.VMEM((TM, d), a.dtype), pltpu.SemaphoreType.DMA((2,))]),
        compiler_params=_cp("arbitrary"),
        name="moe_dispatch",
    )(pos, pad_rows, n_valid, a)


def _experts_kernel(te_ref, xi_ref, nv_ref, x_ref, w1_ref, w3_ref, w2_ref, y_ref, hm_ref):
    @pl.when(pl.program_id(0) < nv_ref[0])
    def _():
        _swiglu_hidden(x_ref[...].astype(BF16), w1_ref.at[0], w3_ref.at[0], hm_ref)
        y_ref[...] = _dot(hm_ref[...], w2_ref[0])

    @pl.when(pl.program_id(0) >= nv_ref[0])
    def _():
        y_ref[...] = jnp.zeros_like(y_ref)


def _moe_experts(xs, tile_expert, x_index, n_valid, w1, w3, w2, n_tiles):
    dp = xs.shape[-1]
    d, f = w1.shape[-2:]
    assert dp == d
    rows = pl.BlockSpec((TM, dp), lambda i, te, xi, nv: (xi[i], 0))
    wspec = lambda s: pl.BlockSpec((1,) + s, lambda i, te, xi, nv: (te[i], 0, 0))
    return pl.pallas_call(
        _experts_kernel,
        out_shape=jax.ShapeDtypeStruct((n_tiles * TM, dp), xs.dtype),
        grid_spec=pltpu.PrefetchScalarGridSpec(
            num_scalar_prefetch=3, grid=(n_tiles,),
            in_specs=[rows, wspec((d, f)), wspec((d, f)), wspec((f, d))],
            out_specs=pl.BlockSpec((TM, dp), lambda i, te, xi, nv: (i, 0)),
            scratch_shapes=[pltpu.VMEM((TM, f), BF16)]),
        compiler_params=_cp("arbitrary"),
        name="moe_experts",
    )(tile_expert, x_index, n_valid, xs, w1, w3, w2)


COMBINE_GROUPS = 4


def _combine_kernel(pos_ref, h_ref, mod_ref, rw_ref, ys_hbm, o_ref, y1_ref, y2_ref, sems, *, nt, fg_ref=None):
    base = (pl.program_id(0) * nt + pl.program_id(1)) * TM

    group = TM // COMBINE_GROUPS
    for r in range(TM):
        t = base + r
        g = r // group
        _row_copy(ys_hbm, pos_ref[2 * t], y1_ref, r, sems.at[2 * g]).start(priority=0)
        _row_copy(ys_hbm, pos_ref[2 * t + 1], y2_ref, r, sems.at[2 * g + 1]).start(priority=1)
    for g in range(COMBINE_GROUPS):
        rows = pl.ds(g * group, group)
        pltpu.make_async_copy(ys_hbm.at[pl.ds(0, group)], y1_ref.at[rows], sems.at[2 * g]).wait()
        pltpu.make_async_copy(ys_hbm.at[pl.ds(0, group)], y2_ref.at[rows], sems.at[2 * g + 1]).wait()
        rw = rw_ref[0, rows, :]
        y = rw[:, 0:1] * y1_ref[rows, :] + rw[:, 1:2] * y2_ref[rows, :]
        out = h_ref[0, rows, :] + mod_ref[0][5:6] * y
        if fg_ref is not None:
            out = out * lax.rsqrt(jnp.mean(out * out, axis=-1, keepdims=True) + EPS) * fg_ref[...]
        o_ref[0, rows, :] = out


def _combine_final_kernel(pos_ref, h_ref, mod_ref, rw_ref, fg_ref, ys_hbm, o_ref, y1_ref, y2_ref, sems, *, nt):
    _combine_kernel(pos_ref, h_ref, mod_ref, rw_ref, ys_hbm, o_ref, y1_ref, y2_ref, sems, nt=nt, fg_ref=fg_ref)


def _moe_combine(ys, pos, rw, h, mods, ctx_tiles, t0, final_g=None):
    nb, l, d = h.shape
    nt = l // TM - t0
    tok = lambda width, off: pl.BlockSpec((1, TM, width), lambda b, t, p: (b, t + off, 0))
    mod = pl.BlockSpec((1, 6, d), lambda b, t, p: (jnp.where(t + t0 < ctx_tiles, nb, b), 0, 0))
    scratch = [pltpu.VMEM((TM, ys.shape[-1]), ys.dtype), pltpu.VMEM((TM, ys.shape[-1]), ys.dtype),
               pltpu.SemaphoreType.DMA((2 * COMBINE_GROUPS,))]
    any_spec = pl.BlockSpec(memory_space=pl.ANY)
    if final_g is None:
        return pl.pallas_call(
            functools.partial(_combine_kernel, nt=nt),
            out_shape=jax.ShapeDtypeStruct((nb, l, d), F32),
            grid_spec=pltpu.PrefetchScalarGridSpec(
                num_scalar_prefetch=1, grid=(nb, nt),
                in_specs=[tok(d, t0), mod, tok(LANES, 0), any_spec],
                out_specs=tok(d, t0), scratch_shapes=scratch),
            input_output_aliases={1: 0},
            compiler_params=_cp("arbitrary", "arbitrary"),
            name="moe_combine",
        )(pos, h, mods, rw, ys)
    return pl.pallas_call(
        functools.partial(_combine_final_kernel, nt=nt),
        out_shape=jax.ShapeDtypeStruct((nb, nt * TM, d), F32),
        grid_spec=pltpu.PrefetchScalarGridSpec(
            num_scalar_prefetch=1, grid=(nb, nt),
            in_specs=[tok(d, t0), mod, tok(LANES, 0), pl.BlockSpec((1, d), lambda b, t, p: (0, 0)), any_spec],
            out_specs=tok(d, 0), scratch_shapes=scratch),
        compiler_params=_cp("arbitrary", "arbitrary"),
        name="moe_combine_final",
    )(pos, h, mods, rw, final_g.reshape(1, d), ys)


def _moe(h, a, ri, rw, counts, mods, n_experts, w1, w3, w2, ctx_tiles, t0, final_g=None):
    nb, n, d = a.shape
    tokens = nb * n
    n_tiles = (TOP_K * tokens) // TM + n_experts
    counts = counts[0, :n_experts]
    padded = ((counts + TM - 1) // TM) * TM
    ends = jnp.cumsum(padded)
    starts = ends - padded
    ri = ri.reshape(tokens, LANES)
    pos = (starts[ri[:, 0:TOP_K]] + ri[:, TOP_K:2 * TOP_K]).reshape(TOP_K * tokens)
    n_valid = ends[-1] // TM
    x_index = jnp.maximum(jnp.minimum(jnp.arange(n_tiles, dtype=jnp.int32), n_valid - 1), 0)
    tile_expert = jnp.minimum(jnp.sum(ends[None, :] <= (x_index * TM)[:, None], axis=1), n_experts - 1)
    fill = jnp.arange(TM, dtype=jnp.int32)[None, :]
    pad_rows = jnp.where(fill < (padded - counts)[:, None], (starts + counts)[:, None] + fill, -1)
    n_valid = n_valid.reshape(1).astype(jnp.int32)
    xs = _moe_dispatch(a, pos.astype(jnp.int32),
                       pad_rows.reshape(-1).astype(jnp.int32), n_valid, n_tiles * TM)
    ys = _moe_experts(xs, tile_expert.astype(jnp.int32), x_index, n_valid, w1, w3, w2, n_tiles)
    return _moe_combine(ys, pos.astype(jnp.int32), rw, h, mods, ctx_tiles, t0, final_g)


def kernel(x, c, ctx, c_ctx, ada_w, ada_b, norm_mix_g, norm_ffn_g, final_g, conv_pw1_w, conv_pw1_b, conv_dw_w, conv_dw_b, conv_ln_g, conv_ln_b, conv_pw2_w, conv_pw2_b, rec_w_in, rec_lb_logits, rec_onorm_g, rec_w_o, ffn_w1, ffn_w3, ffn_w2, moe_router, moe_w1, moe_w3, moe_w2):
    nb, seq, d = x.shape
    n_ctx = ctx.shape[1]
    depth = ada_w.shape[0]
    ctx_tiles = n_ctx // TM
    assert n_ctx % TM == 0 and seq % TM == 0 and n_ctx % BLK == 0 and seq % (GRID_W * 8) == 0
    assert d % (2 * 256) == 0 and nb < 16 and depth % 2 == 0

    cond = jnp.zeros((16, d), F32).at[:nb].set(c).at[nb].set(c_ctx)
    mods = _modulation(cond, ada_w, ada_b)
    sm = jax.nn.softmax(rec_lb_logits.astype(F32), axis=1)
    lower = jnp.cumsum(sm, axis=1) - sm[:, :1]

    bf = lambda w: w.astype(BF16)
    conv_pw1_w, conv_pw2_w, rec_w_in, rec_w_o = bf(conv_pw1_w), bf(conv_pw2_w), bf(rec_w_in), bf(rec_w_o)
    ffn_w1, ffn_w3, ffn_w2 = bf(ffn_w1), bf(ffn_w3), bf(ffn_w2)

    h = jnp.concatenate([ctx, x], axis=1)
    for i in range(depth):
        last = i == depth - 1
        j = i // 2
        t0 = ctx_tiles if last else 0
        if i % 2 == 0:
            u = _pw1_glu(h, mods[i], norm_mix_g[i], conv_pw1_w, j, conv_pw1_b[j], ctx_tiles)
            v = _dwconv(u, conv_dw_w[j], conv_dw_b[j], n_ctx)
            h, e1, e3, e2 = _pw2_ffn(v, h, mods[i], conv_ln_g[j], conv_ln_b[j], conv_pw2_w, j, conv_pw2_b[j],
                                     norm_ffn_g[i], ffn_w1, ffn_w3, ffn_w2, moe_w1, moe_w3, moe_w2, ctx_tiles)
        else:
            p = _rec_in_proj(h, mods[i], norm_mix_g[i], rec_w_in, j, lower[:, i], ctx_tiles)
            o_f, o_b = _gla(p, n_ctx)
            h, a, ri, rw, counts = _readout_route(o_f, o_b, p, h, mods[i], rec_onorm_g[j], rec_w_o, j,
                                                  norm_ffn_g[i], moe_router[j], ctx_tiles, t0)
            h = _moe(h, a, ri, rw, counts, mods[i], moe_router.shape[-1], e1, e3, e2,
                     ctx_tiles, t0, final_g if last else None)
    return h
```
